```python
import jax, jax.numpy as jnp
from jax import lax
import numpy as np

D_MODEL = 1024
BATCH = 8
SEQ = 2048
DEPTH = 2

HEAD_DIM = 64
N_ATTN_HEADS = 8
ATTN_WIDTH = N_ATTN_HEADS * HEAD_DIM
DILATED_PATTERNS = ((128, 1), (512, 4), (2048, 16))
ROPE_THETA = 10000.0
NEG_INF = -1e30

N_DELTA_HEADS = 4
DELTA_DK = 128
DELTA_DV = 128
DELTA_K_WIDTH = N_DELTA_HEADS * DELTA_DK
DELTA_V_WIDTH = N_DELTA_HEADS * DELTA_DV
DELTA_CONV = 4
DELTA_CHUNK = 64

MIX_WIDTH = ATTN_WIDTH + DELTA_V_WIDTH
IN_COLS = 3 * ATTN_WIDTH + 2 * DELTA_K_WIDTH + 2 * DELTA_V_WIDTH + 2 * N_DELTA_HEADS

D_FF = 2816
FFN_CONV = 3
EPS = 1e-6

kernel_name = 'hybrid_dilated_swa_gated_deltanet_convglu'


def rms_norm(x, w):
    xf = x.astype(jnp.float32)
    y = xf * lax.rsqrt(jnp.mean(xf * xf, axis=-1, keepdims=True) + EPS)
    return (y * w.astype(jnp.float32)).astype(x.dtype)


def rope_tables(seq, dim):
    inv = 1.0 / (ROPE_THETA ** (jnp.arange(0, dim, 2, dtype=jnp.float32) / dim))
    ang = jnp.arange(seq, dtype=jnp.float32)[:, None] * inv[None, :]
    return jnp.cos(ang), jnp.sin(ang)


def apply_rope(x, cos, sin):
    x1, x2 = jnp.split(x, 2, axis=-1)
    c = cos[None, :, None, :]
    s = sin[None, :, None, :]
    return jnp.concatenate([x1 * c - x2 * s, x1 * s + x2 * c], axis=-1)


def causal_dwconv(x, w):
    K, C = w.shape
    return lax.conv_general_dilated(
        x, w[:, None, :].astype(x.dtype), window_strides=(1,), padding=[(K - 1, 0)],
        dimension_numbers=('NWC', 'WIO', 'NWC'), feature_group_count=C)


def dilated_branch(q, k, v, window, dilation):
    B, S, H, D = q.shape
    n = window // dilation
    L = S // dilation
    nb = -(-L // n)
    Lp = nb * n

    def split(t):
        return t.reshape(B, L, dilation, H, D).transpose(0, 2, 3, 1, 4)

    qs = jnp.pad(split(q), ((0, 0), (0, 0), (0, 0), (0, Lp - L), (0, 0)))
    qs = qs.reshape(B, dilation, H, nb, n, D)

    def kv_blocks(t):
        tp = jnp.pad(split(t), ((0, 0), (0, 0), (0, 0), (n, Lp - L), (0, 0)))
        prev = tp[:, :, :, :Lp].reshape(B, dilation, H, nb, n, D)
        cur = tp[:, :, :, n:].reshape(B, dilation, H, nb, n, D)
        return jnp.concatenate([prev, cur], axis=4)

    kb = kv_blocks(k)
    vb = kv_blocks(v)
    s = jnp.einsum('brhnqd,brhnkd->brhnqk', qs, kb) * (D ** -0.5)
    a = jnp.arange(n)[:, None]
    c = jnp.arange(2 * n)[None, :]
    key_pos = (jnp.arange(nb)[:, None, None] - 1) * n + c[None]
    valid = ((c >= a) & (c <= a + n))[None] & (key_pos >= 0)
    s = jnp.where(valid, s, NEG_INF)
    m = jnp.max(s, axis=-1, keepdims=True)
    e = jnp.exp(s - m)
    l = jnp.sum(e, axis=-1, keepdims=True)
    o = jnp.einsum('brhnqk,brhnkd->brhnqd', e, vb) / l
    lse = (m + jnp.log(l))[..., 0]

    o = o.reshape(B, dilation, H, Lp, D)[:, :, :, :L]
    o = o.transpose(0, 3, 1, 2, 4).reshape(B, S, H, D)
    lse = lse.reshape(B, dilation, H, Lp)[:, :, :, :L]
    lse = lse.transpose(0, 3, 1, 2).reshape(B, S, H)
    return o, lse


def dilated_attention(q, k, v):
    outs, lses = [], []
    for window, dilation in DILATED_PATTERNS:
        o, lse = dilated_branch(q, k, v, window, dilation)
        outs.append(o)
        lses.append(lse)
    wts = jax.nn.softmax(jnp.stack(lses, axis=0), axis=0)
    return jnp.sum(wts[..., None] * jnp.stack(outs, axis=0), axis=0)


def gated_delta_rule(q, k, v, beta, g):
    B, S, H, K = q.shape
    V = v.shape[-1]
    C = DELTA_CHUNK
    N = S // C
    q = q * lax.rsqrt(jnp.sum(q * q, axis=-1, keepdims=True) + EPS) * (K ** -0.5)
    k = k * lax.rsqrt(jnp.sum(k * k, axis=-1, keepdims=True) + EPS)

    def chunk(t):
        return jnp.moveaxis(t.reshape((B, N, C, H) + t.shape[3:]), 3, 1)

    q, k, v, beta, g = chunk(q), chunk(k), chunk(v), chunk(beta), chunk(g)
    g = jnp.cumsum(g, axis=-1)
    tril = jnp.tril(jnp.ones((C, C), dtype=bool))
    strict = jnp.tril(jnp.ones((C, C), dtype=bool), -1)
    decay = jnp.exp(jnp.where(tril, g[..., :, None] - g[..., None, :], -jnp.inf))
    kb = k * beta[..., None]
    A = jnp.where(strict, jnp.einsum('bhnik,bhnjk->bhnij', kb, k) * decay, 0.0)
    eye = jnp.eye(C, dtype=A.dtype)
    T = lax.linalg.triangular_solve(A + eye, jnp.broadcast_to(eye, A.shape), left_side=True,
                                    lower=True, unit_diagonal=True)
    u = jnp.einsum('bhnij,bhnjv->bhniv', T, v * beta[..., None])
    w = jnp.einsum('bhnij,bhnjk->bhnik', T, kb * jnp.exp(g)[..., None])
    qk = jnp.einsum('bhnik,bhnjk->bhnij', q, k) * decay
    q_dec = q * jnp.exp(g)[..., None]
    k_dec = k * jnp.exp(g[..., -1:] - g)[..., None]
    g_last = jnp.exp(g[..., -1])

    def step(state, xs):
        q_i, k_i, u_i, w_i, qk_i, gl = xs
        v_new = u_i - jnp.einsum('bhck,bhkv->bhcv', w_i, state)
        o = jnp.einsum('bhck,bhkv->bhcv', q_i, state) + jnp.einsum('bhij,bhjv->bhiv', qk_i, v_new)
        state = state * gl[..., None, None] + jnp.einsum('bhck,bhcv->bhkv', k_i, v_new)
        return state, o

    xs = (jnp.moveaxis(q_dec, 2, 0), jnp.moveaxis(k_dec, 2, 0), jnp.moveaxis(u, 2, 0),
          jnp.moveaxis(w, 2, 0), jnp.moveaxis(qk, 2, 0), jnp.moveaxis(g_last, 2, 0))
    _, o = lax.scan(step, jnp.zeros((B, H, K, V), jnp.float32), xs)
    return jnp.transpose(o, (1, 0, 3, 2, 4)).reshape(B, S, H, V)


def hybrid_mixer(h, w_in, dn_conv_w, dn_a_log, dn_dt_bias, dn_norm_w, w_out, cos, sin):
    B, S, _ = h.shape
    proj = h @ w_in
    cuts = [ATTN_WIDTH, 2 * ATTN_WIDTH, 3 * ATTN_WIDTH,
            3 * ATTN_WIDTH + 2 * DELTA_K_WIDTH + DELTA_V_WIDTH,
            3 * ATTN_WIDTH + 2 * DELTA_K_WIDTH + 2 * DELTA_V_WIDTH,
            3 * ATTN_WIDTH + 2 * DELTA_K_WIDTH + 2 * DELTA_V_WIDTH + N_DELTA_HEADS]
    aq, ak, av, dn_qkv, dn_z, dn_b, dn_a = jnp.split(proj, cuts, axis=-1)

    aq = apply_rope(aq.reshape(B, S, N_ATTN_HEADS, HEAD_DIM).astype(jnp.float32), cos, sin)
    ak = apply_rope(ak.reshape(B, S, N_ATTN_HEADS, HEAD_DIM).astype(jnp.float32), cos, sin)
    av = av.reshape(B, S, N_ATTN_HEADS, HEAD_DIM).astype(jnp.float32)
    attn_out = dilated_attention(aq, ak, av).reshape(B, S, ATTN_WIDTH).astype(h.dtype)

    qkv = jax.nn.silu(causal_dwconv(dn_qkv, dn_conv_w)).astype(jnp.float32)
    dq, dk, dv = jnp.split(qkv, [DELTA_K_WIDTH, 2 * DELTA_K_WIDTH], axis=-1)
    beta = jax.nn.sigmoid(dn_b.astype(jnp.float32))
    g = -jnp.exp(dn_a_log.astype(jnp.float32)) * jax.nn.softplus(
        dn_a.astype(jnp.float32) + dn_dt_bias.astype(jnp.float32))
    o = gated_delta_rule(dq.reshape(B, S, N_DELTA_HEADS, DELTA_DK),
                         dk.reshape(B, S, N_DELTA_HEADS, DELTA_DK),
                         dv.reshape(B, S, N_DELTA_HEADS, DELTA_DV), beta, g)
    z = dn_z.reshape(B, S, N_DELTA_HEADS, DELTA_DV).astype(jnp.float32)
    dn_out = (rms_norm(o, dn_norm_w) * jax.nn.silu(z)).reshape(B, S, DELTA_V_WIDTH).astype(h.dtype)

    return jnp.concatenate([attn_out, dn_out], axis=-1) @ w_out


def conv_glu_ffn(h, ffn_w_in, ffn_conv_w, ffn_conv_b, ffn_w_out):
    u = causal_dwconv(h @ ffn_w_in, ffn_conv_w) + ffn_conv_b
    gate, up = jnp.split(u, 2, axis=-1)
    return (jax.nn.gelu(gate, approximate=True) * up) @ ffn_w_out


def setup_inputs(seed: int = 0) -> dict:
    key = jax.random.key(seed)
    ks = jax.random.split(key, 16)
    f32 = jnp.float32
    nrm = lambda k, shape, scale: jax.random.normal(k, shape, f32) * scale
    dt = jnp.exp(jax.random.uniform(ks[4], (DEPTH, N_DELTA_HEADS), f32) *
                 (jnp.log(0.1) - jnp.log(0.001)) + jnp.log(0.001))
    return {
        'x': nrm(ks[0], (BATCH, SEQ, D_MODEL), 1.0),
        'w_in': nrm(ks[1], (DEPTH, D_MODEL, IN_COLS), D_MODEL ** -0.5),
        'dn_conv_w': nrm(ks[2], (DEPTH, DELTA_CONV, 2 * DELTA_K_WIDTH + DELTA_V_WIDTH), DELTA_CONV ** -0.5),
        'dn_a_log': jnp.log(jax.random.uniform(ks[3], (DEPTH, N_DELTA_HEADS), f32, 1.0, 16.0)),
        'dn_dt_bias': dt + jnp.log(-jnp.expm1(-dt)),
        'dn_norm_w': 1.0 + nrm(ks[5], (DEPTH, DELTA_DV), 0.05),
        'w_out': nrm(ks[6], (DEPTH, MIX_WIDTH, D_MODEL), MIX_WIDTH ** -0.5),
        'ffn_w_in': nrm(ks[7], (DEPTH, D_MODEL, 2 * D_FF), D_MODEL ** -0.5),
        'ffn_conv_w': nrm(ks[8], (DEPTH, FFN_CONV, 2 * D_FF), FFN_CONV ** -0.5),
        'ffn_conv_b': nrm(ks[9], (DEPTH, 2 * D_FF), 0.01),
        'ffn_w_out': nrm(ks[10], (DEPTH, D_FF, D_MODEL), D_FF ** -0.5),
        'norm_pre_mix': 1.0 + nrm(ks[11], (DEPTH, D_MODEL), 0.05),
        'norm_post_mix': 1.0 + nrm(ks[12], (DEPTH, D_MODEL), 0.05),
        'norm_pre_ffn': 1.0 + nrm(ks[13], (DEPTH, D_MODEL), 0.05),
        'norm_post_ffn': 1.0 + nrm(ks[14], (DEPTH, D_MODEL), 0.05),
    }


def reference(x, w_in, dn_conv_w, dn_a_log, dn_dt_bias, dn_norm_w, w_out, ffn_w_in, ffn_conv_w,
              ffn_conv_b, ffn_w_out, norm_pre_mix, norm_post_mix, norm_pre_ffn, norm_post_ffn):
    cos, sin = rope_tables(x.shape[1], HEAD_DIM)
    for l in range(DEPTH):
        h = rms_norm(x, norm_pre_mix[l])
        mix = hybrid_mixer(h, w_in[l], dn_conv_w[l], dn_a_log[l], dn_dt_bias[l], dn_norm_w[l],
                           w_out[l], cos, sin)
        x = x + rms_norm(mix, norm_post_mix[l]).astype(x.dtype)
        h = rms_norm(x, norm_pre_ffn[l])
        f = conv_glu_ffn(h, ffn_w_in[l], ffn_conv_w[l], ffn_conv_b[l], ffn_w_out[l])
        x = x + rms_norm(f, norm_post_ffn[l]).astype(x.dtype)
    return x
```

```python
import functools
import math

import numpy as np
import jax
import jax.numpy as jnp
from jax import lax
from jax.experimental import pallas as pl
from jax.experimental.pallas import tpu as pltpu

F32 = jnp.float32
BF16 = jnp.bfloat16

HEAD_DIM = 64
N_ATTN_HEADS = 8
ATTN_WIDTH = N_ATTN_HEADS * HEAD_DIM
DILATED_PATTERNS = ((128, 1), (512, 4), (2048, 16))
ROPE_THETA = 10000.0
MASK_VALUE = -1e30

N_DELTA_HEADS = 4
DELTA_DK = 128
DELTA_DV = 128
DELTA_WIDTH = N_DELTA_HEADS * DELTA_DK
DELTA_CONV = 4
DELTA_CHUNK = 64

MAIN_COLS = 3 * ATTN_WIDTH + 4 * DELTA_WIDTH
GATE_COLS = 2 * N_DELTA_HEADS
FFN_CONV = 3
EPS = 1e-6

LANES = 128
BF16_SUBLANES = 16
VMEM_LIMIT = 56 * 1024 * 1024

ROW_TILE = 512
PROJ_COL_TILE = 512
FFN_COL_TILE = 256
ATTN_TILE = 256


def _resident(shape):
    nd = len(shape)
    return pl.BlockSpec(shape, lambda *_: (0,) * nd, pipeline_mode=pl.Buffered(1))


def _params(n_grid):
    return pltpu.CompilerParams(dimension_semantics=("arbitrary",) * n_grid,
                                vmem_limit_bytes=VMEM_LIMIT)


def _rms(x, w):
    return x * lax.rsqrt(jnp.mean(x * x, axis=-1, keepdims=True) + EPS) * w


def _sigmoid(x):
    return 1.0 / (1.0 + jnp.exp(-x))


def _inproj_body(x_ref, nw_ref, w_ref, wg_ref, proj_ref, gates_ref):
    h = _rms(x_ref[...], nw_ref[...]).astype(BF16)
    for c in range(proj_ref.shape[1] // PROJ_COL_TILE):
        cs = slice(c * PROJ_COL_TILE, (c + 1) * PROJ_COL_TILE)
        proj_ref[:, cs] = jnp.dot(h, w_ref[:, cs], preferred_element_type=F32).astype(BF16)
    gates_ref[...] = jnp.dot(h, wg_ref[...], preferred_element_type=F32)


def _inproj(x, nw, w_main, w_gate):
    tok, d = x.shape
    return pl.pallas_call(
        _inproj_body,
        out_shape=(jax.ShapeDtypeStruct((tok, MAIN_COLS), BF16),
                   jax.ShapeDtypeStruct((tok, LANES), F32)),
        grid=(tok // ROW_TILE,),
        in_specs=[pl.BlockSpec((ROW_TILE, d), lambda i: (i, 0)),
                  _resident((1, d)),
                  _resident((d, MAIN_COLS)),
                  _resident((d, LANES))],
        out_specs=(pl.BlockSpec((ROW_TILE, MAIN_COLS), lambda i: (i, 0)),
                   pl.BlockSpec((ROW_TILE, LANES), lambda i: (i, 0))),
        compiler_params=_params(1),
        name="inproj",
    )(x, nw, w_main, w_gate)


def _attn_bias_tiles(seq, t):
    a = np.arange(t)[:, None]
    c = np.arange(t)[None, :]
    tiles = []
    for d in range(seq // t):
        delta = d * t + a - c
        mult = np.zeros((t, t), np.int64)
        for window, dil in DILATED_PATTERNS:
            mult += (delta >= 0) & (delta % dil == 0) & (delta <= window)
        tiles.append(np.where(mult > 0, np.log(np.maximum(mult, 1)), MASK_VALUE).astype(np.float32))
    n_distinct = 1
    while not all(np.array_equal(tiles[n_distinct - 1], tl) for tl in tiles[n_distinct:]):
        n_distinct += 1
    return np.stack(tiles[:n_distinct])


def _rope(x, cos, sin_signed):
    lane = lax.broadcasted_iota(jnp.int32, (1, LANES), 1)
    partner = jnp.where((lane % HEAD_DIM) < HEAD_DIM // 2,
                        pltpu.roll(x, LANES - HEAD_DIM // 2, axis=1),
                        pltpu.roll(x, HEAD_DIM // 2, axis=1))
    return x * cos + partner * sin_signed


def _attn_body(q_ref, k_ref, v_ref, cq_ref, sq_ref, ck_ref, sk_ref, bias_ref, o_ref,
               kt_scr, q_scr, m_scr, l_scr, acc_scr, *, n_bias):
    t = ATTN_TILE
    i = pl.program_id(2)
    lane = lax.broadcasted_iota(jnp.int32, (1, LANES), 1)
    head0 = lane < HEAD_DIM

    @pl.when(i == 0)
    def _():
        for c in range(k_ref.shape[0] // t):
            rs = slice(c * t, (c + 1) * t)
            k = _rope(k_ref[rs, :].astype(F32), ck_ref[rs, :], sk_ref[rs, :])
            kt_scr[:, rs] = k.T.astype(BF16)

    q = _rope(q_ref[...].astype(F32), cq_ref[...], sq_ref[...]) * (HEAD_DIM ** -0.5)
    q_scr[0] = jnp.where(head0, q, 0.0).astype(BF16)
    q_scr[1] = jnp.where(head0, 0.0, q).astype(BF16)
    m_scr[...] = jnp.full(m_scr.shape, MASK_VALUE, F32)
    l_scr[...] = jnp.zeros(l_scr.shape, F32)
    acc_scr[...] = jnp.zeros(acc_scr.shape, F32)

    def step(j, _):
        bias = bias_ref[jnp.minimum(i - j, n_bias - 1)]
        col = pl.multiple_of(j * t, t)
        ktj = kt_scr[:, pl.ds(col, t)]
        vj = v_ref[pl.ds(col, t), :]
        alphas, pvs = [], []
        for hh in range(2):
            s = jnp.dot(q_scr[hh], ktj, preferred_element_type=F32) + bias
            m_prev = m_scr[hh]
            m_new = jnp.maximum(m_prev, jnp.max(s, axis=1, keepdims=True))
            p = jnp.exp(s - m_new)
            alpha = jnp.exp(m_prev - m_new)
            l_scr[hh] = alpha * l_scr[hh] + jnp.sum(p, axis=1, keepdims=True)
            m_scr[hh] = m_new
            alphas.append(alpha)
            pvs.append(jnp.dot(p.astype(BF16), vj, preferred_element_type=F32))
        acc_scr[...] = (acc_scr[...] * jnp.where(head0, alphas[0], alphas[1])
                        + jnp.where(head0, pvs[0], pvs[1]))
        return 0

    lax.fori_loop(0, i + 1, step, 0)
    inv = jnp.where(head0, 1.0 / l_scr[0], 1.0 / l_scr[1])
    o_ref[...] = (acc_scr[...] * inv).astype(BF16)


def _attention(proj, cos, sin_signed, batch, seq):
    t = ATTN_TILE
    nq = seq // t
    bias = _attn_bias_tiles(seq, t)
    n_pairs = ATTN_WIDTH // LANES
    return pl.pallas_call(
        functools.partial(_attn_body, n_bias=bias.shape[0]),
        out_shape=jax.ShapeDtypeStruct((batch * seq, ATTN_WIDTH), BF16),
        grid=(batch, n_pairs, nq),
        in_specs=[pl.BlockSpec((t, LANES), lambda b, p, i: (b * nq + i, p)),
                  pl.BlockSpec((seq, LANES), lambda b, p, i: (b, n_pairs + p)),
                  pl.BlockSpec((seq, LANES), lambda b, p, i: (b, 2 * n_pairs + p)),
                  pl.BlockSpec((t, LANES), lambda b, p, i: (i, 0)),
                  pl.BlockSpec((t, LANES), lambda b, p, i: (i, 0)),
                  _resident((seq, LANES)),
                  _resident((seq, LANES)),
                  _resident(bias.shape)],
        out_specs=pl.BlockSpec((t, LANES), lambda b, p, i: (b * nq + i, p)),
        scratch_shapes=[pltpu.VMEM((LANES, seq), BF16),
                        pltpu.VMEM((2, t, LANES), BF16),
                        pltpu.VMEM((2, t, 1), F32),
                        pltpu.VMEM((2, t, 1), F32),
                        pltpu.VMEM((t, LANES), F32)],
        compiler_params=_params(3),
        name="dilated_attention",
    )(proj, proj, proj, cos, sin_signed, cos, sin_signed, jnp.asarray(bias))


def _dot_nt(a, b, **kw):
    return lax.dot_general(a, b, (((1,), (1,)), ((), ())), preferred_element_type=F32, **kw)


def _dot_tn(a, b, **kw):
    return lax.dot_general(a, b, (((0,), (0,)), ((), ())), preferred_element_type=F32, **kw)


def _unit_lower_inverse(a):
    c = a.shape[0]
    hi = lax.Precision.HIGHEST
    row = lax.broadcasted_iota(jnp.int32, (c, c), 0)
    col = lax.broadcasted_iota(jnp.int32, (c, c), 1)
    x = jnp.where(row == col, 1.0, 0.0) - a
    p = a
    for _ in range(int(math.log2(c)) - 1):
        p = jnp.dot(p, p, precision=hi, preferred_element_type=F32)
        x = x + jnp.dot(x, p, precision=hi, preferred_element_type=F32)
    return x


def _delta_body(q_ref, k_ref, v_ref, z_ref, gates_ref, cw_ref, alog_ref, dtb_ref, nw_ref, o_ref,
                lhs_scr, b_scr, u_scr, qk_scr, gl_scr, state_scr):
    C = DELTA_CHUNK
    H = N_DELTA_HEADS
    n_chunks = q_ref.shape[0] // C
    halo = BF16_SUBLANES
    row = lax.broadcasted_iota(jnp.int32, (C, C), 0)
    col = lax.broadcasted_iota(jnp.int32, (C, C), 1)
    tril = row >= col
    strict = row > col
    ones_tril = jnp.where(tril, 1.0, 0.0)

    def conv_silu(ref, part, h, r0, halo_on):
        cs = slice(h * LANES, (h + 1) * LANES)
        hs = pl.multiple_of(jnp.maximum(r0 - halo, 0), halo)
        xh = jnp.where(halo_on, ref[pl.ds(hs, halo), cs].astype(F32), 0.0)
        xc = ref[pl.ds(r0, C), cs].astype(F32)
        xcat = jnp.concatenate([xh, xc], axis=0)
        wc = part * DELTA_WIDTH + h * LANES
        y = xc * cw_ref[DELTA_CONV - 1:DELTA_CONV, wc:wc + LANES]
        for s in range(1, DELTA_CONV):
            tap = DELTA_CONV - 1 - s
            y = y + pltpu.roll(xcat, s, axis=0)[halo:] * cw_ref[tap:tap + 1, wc:wc + LANES]
        return y * _sigmoid(y)

    def prepare(n, slot):
        r0 = pl.multiple_of(n * C, C)
        halo_on = n > 0
        raw = gates_ref[pl.ds(r0, C), :]
        beta_all = _sigmoid(raw)
        sp_in = raw + dtb_ref[...]
        softplus = jnp.maximum(sp_in, 0.0) + jnp.log(1.0 + jnp.exp(-jnp.abs(sp_in)))
        g_step = -jnp.exp(alog_ref[...]) * softplus
        gc = jnp.dot(ones_tril, g_step, precision=lax.Precision.HIGHEST,
                     preferred_element_type=F32)
        gc_t = gc.T
        for h in range(H):
            q = conv_silu(q_ref, 0, h, r0, halo_on)
            k = conv_silu(k_ref, 1, h, r0, halo_on)
            v = conv_silu(v_ref, 2, h, r0, halo_on)
            q = q * lax.rsqrt(jnp.sum(q * q, axis=-1, keepdims=True) + EPS) * (DELTA_DK ** -0.5)
            k = k * lax.rsqrt(jnp.sum(k * k, axis=-1, keepdims=True) + EPS)
            gi = gc[:, H + h:H + h + 1]
            gj = gc_t[H + h:H + h + 1, :]
            g_last = gc[C - 1:C, H + h:H + h + 1]
            beta = beta_all[:, h:h + 1]
            decay = jnp.exp(jnp.where(tril, gi - gj, -jnp.inf))
            kb = k * beta
            a = jnp.where(strict, _dot_nt(kb, k) * decay, 0.0)
            t_inv = _unit_lower_inverse(a)
            eg = jnp.exp(gi)
            uw = jnp.dot(t_inv, jnp.concatenate([v * beta, kb * eg], axis=1),
                         preferred_element_type=F32)
            u = uw[:, :DELTA_DV]
            w = uw[:, DELTA_DV:]
            qk = _dot_nt(q, k) * decay
            kd = k * jnp.exp(g_last - gi)
            gb = _dot_tn(kd, jnp.concatenate([w, u], axis=1))
            lhs_scr[slot, h] = jnp.concatenate([gb[:, :DELTA_DK], w, q * eg], axis=0).astype(BF16)
            b_scr[slot, h] = gb[:, DELTA_DK:]
            u_scr[slot, h] = u
            qk_scr[slot, h] = qk
            gl_scr[slot, h] = jnp.broadcast_to(jnp.exp(g_last), gl_scr.shape[2:])

    def consume(n, slot):
        r0 = pl.multiple_of(n * C, C)
        for h in range(H):
            cs = slice(h * LANES, (h + 1) * LANES)
            s_old = state_scr[h]
            r = jnp.dot(lhs_scr[slot, h], s_old.astype(BF16), preferred_element_type=F32)
            v_new = u_scr[slot, h] - r[DELTA_DK:DELTA_DK + C]
            o = r[DELTA_DK + C:] + jnp.dot(qk_scr[slot, h], v_new, preferred_element_type=F32)
            state_scr[h] = gl_scr[slot, h][0:1, :] * s_old + b_scr[slot, h] - r[:DELTA_DK]
            z = z_ref[pl.ds(r0, C), cs].astype(F32)
            o_ref[pl.ds(r0, C), cs] = (_rms(o, nw_ref[...]) * z * _sigmoid(z)).astype(BF16)

    state_scr[...] = jnp.zeros(state_scr.shape, F32)
    prepare(0, 0)

    def step(n, _):
        slot = n % 2
        consume(n, slot)
        prepare(jnp.minimum(n + 1, n_chunks - 1), 1 - slot)
        return 0

    lax.fori_loop(0, n_chunks, step, 0)


def _delta(proj, gates, conv_w, alog, dtb, norm_w, batch, seq):
    C = DELTA_CHUNK
    H = N_DELTA_HEADS
    first = 3 * ATTN_WIDTH // DELTA_WIDTH
    blk = lambda j: pl.BlockSpec((seq, DELTA_WIDTH), lambda b: (b, first + j))
    return pl.pallas_call(
        _delta_body,
        out_shape=jax.ShapeDtypeStruct((batch * seq, DELTA_WIDTH), BF16),
        grid=(batch,),
        in_specs=[blk(0), blk(1), blk(2), blk(3),
                  pl.BlockSpec((seq, LANES), lambda b: (b, 0)),
                  _resident(conv_w.shape), _resident((1, LANES)), _resident((1, LANES)),
                  _resident((1, DELTA_DV))],
        out_specs=pl.BlockSpec((seq, DELTA_WIDTH), lambda b: (b, 0)),
        scratch_shapes=[pltpu.VMEM((2, H, DELTA_DK + 2 * C, DELTA_DK), BF16),
                        pltpu.VMEM((2, H, DELTA_DK, DELTA_DV), F32),
                        pltpu.VMEM((2, H, C, DELTA_DV), F32),
                        pltpu.VMEM((2, H, C, C), F32),
                        pltpu.VMEM((2, H, 8, DELTA_DV), F32),
                        pltpu.VMEM((H, DELTA_DK, DELTA_DV), F32)],
        compiler_params=_params(1),
        name="gated_delta",
    )(proj, proj, proj, proj, gates, conv_w, alog, dtb, norm_w)


def _outproj_body(a_ref, d_ref, x_ref, w_ref, nw_ref, o_ref):
    wa = a_ref.shape[1]
    mix = (jnp.dot(a_ref[...], w_ref[:wa, :], preferred_element_type=F32)
           + jnp.dot(d_ref[...], w_ref[wa:, :], preferred_element_type=F32))
    o_ref[...] = x_ref[...] + _rms(mix, nw_ref[...])


def _outproj(attn, dn, x, w, nw):
    tok, d = x.shape
    return pl.pallas_call(
        _outproj_body,
        out_shape=jax.ShapeDtypeStruct((tok, d), F32),
        grid=(tok // ROW_TILE,),
        in_specs=[pl.BlockSpec((ROW_TILE, attn.shape[1]), lambda i: (i, 0)),
                  pl.BlockSpec((ROW_TILE, dn.shape[1]), lambda i: (i, 0)),
                  pl.BlockSpec((ROW_TILE, d), lambda i: (i, 0)),
                  _resident(w.shape), _resident((1, d))],
        out_specs=pl.BlockSpec((ROW_TILE, d), lambda i: (i, 0)),
        compiler_params=_params(1),
        name="outproj",
    )(attn, dn, x, w, nw)


def _gelu_tanh(x):
    return 0.5 * x * (1.0 + jnp.tanh(math.sqrt(2.0 / math.pi) * (x + 0.044715 * (x * x * x))))


def _ffn_body(x_ref, nw_pre_ref, w_in_ref, cw_ref, cb_ref, w_out_ref, nw_post_ref, o_ref,
              h_scr, acc_scr, carry_scr, *, tiles_per_seq):
    tm = x_ref.shape[0]
    d_ff = w_out_ref.shape[0]
    tf = FFN_COL_TILE
    x = x_ref[...]
    h_scr[...] = _rms(x, nw_pre_ref[...]).astype(BF16)
    acc_scr[...] = jnp.zeros(acc_scr.shape, F32)
    seq_start = pl.program_id(0) % tiles_per_seq == 0

    def conv_branch(col):
        cs = pl.ds(col, tf)
        u = jnp.dot(h_scr[...], w_in_ref[:, cs], preferred_element_type=F32)
        ucat = jnp.concatenate([jnp.where(seq_start, 0.0, carry_scr[:, cs]), u], axis=0)
        carry_scr[:, cs] = u[tm - 8:, :]
        y = u * cw_ref[FFN_CONV - 1:FFN_CONV, cs] + cb_ref[:, cs]
        for s in range(1, FFN_CONV):
            tap = FFN_CONV - 1 - s
            y = y + pltpu.roll(ucat, s, axis=0)[8:] * cw_ref[tap:tap + 1, cs]
        return y

    def chunk(c, _):
        col = pl.multiple_of(c * tf, tf)
        gate = conv_branch(col)
        up = conv_branch(pl.multiple_of(d_ff + col, tf))
        act = (_gelu_tanh(gate) * up).astype(BF16)
        acc_scr[...] += jnp.dot(act, w_out_ref[pl.ds(col, tf), :], preferred_element_type=F32)
        return 0

    lax.fori_loop(0, d_ff // tf, chunk, 0)
    o_ref[...] = x + _rms(acc_scr[...], nw_post_ref[...])


def _ffn(x, nw_pre, w_in, conv_w, conv_b, w_out, nw_post, seq):
    tok, d = x.shape
    d_ff = w_out.shape[0]
    assert d_ff % FFN_COL_TILE == 0 and seq % ROW_TILE == 0
    return pl.pallas_call(
        functools.partial(_ffn_body, tiles_per_seq=seq // ROW_TILE),
        out_shape=jax.ShapeDtypeStruct((tok, d), F32),
        grid=(tok // ROW_TILE,),
        in_specs=[pl.BlockSpec((ROW_TILE, d), lambda i: (i, 0)),
                  _resident((1, d)), _resident(w_in.shape), _resident(conv_w.shape),
                  _resident((1, 2 * d_ff)), _resident(w_out.shape), _resident((1, d))],
        out_specs=pl.BlockSpec((ROW_TILE, d), lambda i: (i, 0)),
        scratch_shapes=[pltpu.VMEM((ROW_TILE, d), BF16),
                        pltpu.VMEM((ROW_TILE, d), F32),
                        pltpu.VMEM((8, 2 * d_ff), F32)],
        compiler_params=_params(1),
        name="conv_glu_ffn",
    )(x, nw_pre, w_in, conv_w, conv_b, w_out, nw_post)


def _rope_tables(seq):
    half = HEAD_DIM // 2
    inv = 1.0 / (ROPE_THETA ** (jnp.arange(0, HEAD_DIM, 2, dtype=F32) / HEAD_DIM))
    ang = jnp.arange(seq, dtype=F32)[:, None] * inv[None, :]
    cos, sin = jnp.cos(ang), jnp.sin(ang)
    reps = LANES // half
    sign = jnp.tile(jnp.concatenate([-jnp.ones((half,), F32), jnp.ones((half,), F32)]), reps // 2)
    return jnp.tile(cos, (1, reps)), jnp.tile(sin, (1, reps)) * sign[None, :]


def _lane_row(vals, offset):
    return jnp.zeros((1, LANES), F32).at[0, offset:offset + vals.shape[0]].set(vals.astype(F32))


def kernel(x, w_in, dn_conv_w, dn_a_log, dn_dt_bias, dn_norm_w, w_out, ffn_w_in, ffn_conv_w,
           ffn_conv_b, ffn_w_out, norm_pre_mix, norm_post_mix, norm_pre_ffn, norm_post_ffn):
    batch, seq, d = x.shape
    depth = w_in.shape[0]
    assert w_in.shape[2] == MAIN_COLS + GATE_COLS
    cos, sin_signed = _rope_tables(seq)
    xt = x.reshape(batch * seq, d)
    for l in range(depth):
        w_main = w_in[l, :, :MAIN_COLS].astype(BF16)
        w_gate = jnp.pad(w_in[l, :, MAIN_COLS:], ((0, 0), (0, LANES - GATE_COLS))).astype(BF16)
        proj, gates = _inproj(xt, norm_pre_mix[l][None, :], w_main, w_gate)
        attn = _attention(proj, cos, sin_signed, batch, seq)
        dn = _delta(proj, gates, dn_conv_w[l],
                    _lane_row(dn_a_log[l], N_DELTA_HEADS), _lane_row(dn_dt_bias[l], N_DELTA_HEADS),
                    dn_norm_w[l][None, :], batch, seq)
        xt = _outproj(attn, dn, xt, w_out[l].astype(BF16), norm_post_mix[l][None, :])
        xt = _ffn(xt, norm_pre_ffn[l][None, :], ffn_w_in[l].astype(BF16), ffn_conv_w[l],
                  ffn_conv_b[l][None, :], ffn_w_out[l].astype(BF16), norm_post_ffn[l][None, :], seq)
    return xt.reshape(batch, seq, d)
```

```python
import functools
import math

import numpy as np
import jax
import jax.numpy as jnp
from jax import lax
from jax.experimental import pallas as pl
from jax.experimental.pallas import tpu as pltpu

F32 = jnp.float32
BF16 = jnp.bfloat16

HEAD_DIM = 64
N_ATTN_HEADS = 8
ATTN_WIDTH = N_ATTN_HEADS * HEAD_DIM
DILATED_PATTERNS = ((128, 1), (512, 4), (2048, 16))
ROPE_THETA = 10000.0
MASK_VALUE = -1e30

N_DELTA_HEADS = 4
DELTA_DK = 128
DELTA_DV = 128
DELTA_WIDTH = N_DELTA_HEADS * DELTA_DK
DELTA_CONV = 4
DELTA_CHUNK = 64

MAIN_COLS = 3 * ATTN_WIDTH + 4 * DELTA_WIDTH
GATE_COLS = 2 * N_DELTA_HEADS
FFN_CONV = 3
EPS = 1e-6

LANES = 128
BF16_SUBLANES = 16
VMEM_LIMIT = 56 * 1024 * 1024

ROW_TILE = 512
PROJ_COL_TILE = 512
FFN_COL_TILE = 256
ATTN_TILE = 512
ATTN_PAIRS_PER_STEP = 2


def _resident(shape):
    nd = len(shape)
    return pl.BlockSpec(shape, lambda *_: (0,) * nd, pipeline_mode=pl.Buffered(1))


def _params(n_grid):
    return pltpu.CompilerParams(dimension_semantics=("arbitrary",) * n_grid,
                                vmem_limit_bytes=VMEM_LIMIT)


def _rms(x, w):
    return x * lax.rsqrt(jnp.mean(x * x, axis=-1, keepdims=True) + EPS) * w


def _sigmoid(x):
    return 1.0 / (1.0 + jnp.exp(-x))


def _dot(a, b):
    return jnp.dot(a, b, preferred_element_type=F32)


def _inproj_body(x_ref, nw_ref, w_ref, wg_ref, proj_ref, gates_ref):
    h = _rms(x_ref[...], nw_ref[...]).astype(BF16)
    for c in range(proj_ref.shape[1] // PROJ_COL_TILE):
        cs = slice(c * PROJ_COL_TILE, (c + 1) * PROJ_COL_TILE)
        proj_ref[:, cs] = _dot(h, w_ref[:, cs]).astype(BF16)
    gates_ref[...] = _dot(h, wg_ref[...])


def _inproj(x, nw, w_main, w_gate):
    tok, d = x.shape
    return pl.pallas_call(
        _inproj_body,
        out_shape=(jax.ShapeDtypeStruct((tok, MAIN_COLS), BF16),
                   jax.ShapeDtypeStruct((tok, LANES), F32)),
        grid=(tok // ROW_TILE,),
        in_specs=[pl.BlockSpec((ROW_TILE, d), lambda i: (i, 0)),
                  _resident((1, d)),
                  _resident((d, MAIN_COLS)),
                  _resident((d, LANES))],
        out_specs=(pl.BlockSpec((ROW_TILE, MAIN_COLS), lambda i: (i, 0)),
                   pl.BlockSpec((ROW_TILE, LANES), lambda i: (i, 0))),
        compiler_params=_params(1),
        name="inproj",
    )(x, nw, w_main, w_gate)


def _attn_bias_tiles(seq, t):
    a = np.arange(t)[:, None]
    c = np.arange(t)[None, :]
    tiles = []
    for d in range(seq // t):
        delta = d * t + a - c
        mult = np.zeros((t, t), np.int64)
        for window, dil in DILATED_PATTERNS:
            mult += (delta >= 0) & (delta % dil == 0) & (delta <= window)
        tiles.append(np.where(mult > 0, np.log(np.maximum(mult, 1)), MASK_VALUE).astype(np.float32))
    n_distinct = 1
    while not all(np.array_equal(tiles[n_distinct - 1], tl) for tl in tiles[n_distinct:]):
        n_distinct += 1
    return np.stack(tiles[:n_distinct]).transpose(0, 2, 1)


def _rope(x, cos, sin_signed):
    lane = lax.broadcasted_iota(jnp.int32, (1, LANES), 1)
    partner = jnp.where((lane % HEAD_DIM) < HEAD_DIM // 2,
                        pltpu.roll(x, LANES - HEAD_DIM // 2, axis=1),
                        pltpu.roll(x, HEAD_DIM // 2, axis=1))
    return x * cos + partner * sin_signed


def _attn_body(q_ref, k_ref, v_ref, cq_ref, sq_ref, ck_ref, sk_ref, bias_ref, o_ref,
               k_scr, vt_scr, qt_scr, m_scr, acc_scr, *, n_bias):
    t = ATTN_TILE
    n_pairs = qt_scr.shape[0]
    i = pl.program_id(2)
    head0 = lax.broadcasted_iota(jnp.int32, (1, LANES), 1) < HEAD_DIM
    top = lax.broadcasted_iota(jnp.int32, (LANES, 1), 0) < HEAD_DIM

    @pl.when(i == 0)
    def _():
        for g in range(n_pairs):
            cs = slice(g * LANES, (g + 1) * LANES)
            for c in range(k_ref.shape[0] // t):
                rs = slice(c * t, (c + 1) * t)
                k = _rope(k_ref[rs, cs].astype(F32), ck_ref[rs, :], sk_ref[rs, :])
                k_scr[2 * g, rs, :] = jnp.where(head0, k, 0.0).astype(BF16)
                k_scr[2 * g + 1, rs, :] = jnp.where(head0, 0.0, k).astype(BF16)
                vt = v_ref[rs, cs].astype(F32).T
                vt_scr[2 * g, :, rs] = jnp.where(top, vt, 1.0).astype(BF16)
                vt_scr[2 * g + 1, :, rs] = jnp.where(top, 1.0, vt).astype(BF16)

    for g in range(n_pairs):
        cs = slice(g * LANES, (g + 1) * LANES)
        q = _rope(q_ref[:, cs].astype(F32), cq_ref[...], sq_ref[...]) * (HEAD_DIM ** -0.5)
        qt_scr[g] = q.T.astype(BF16)
    m_scr[...] = jnp.full(m_scr.shape, MASK_VALUE, F32)
    acc_scr[...] = jnp.zeros(acc_scr.shape, F32)

    def step(j, _):
        bias = bias_ref[jnp.minimum(i - j, n_bias - 1)]
        col = pl.multiple_of(j * t, t)
        heads = range(2 * n_pairs)
        ks = [k_scr[hd, pl.ds(col, t), :] for hd in heads]
        vts = [vt_scr[hd, :, pl.ds(col, t)] for hd in heads]
        qts = [qt_scr[g] for g in range(n_pairs)]
        m_prev = [m_scr[hd] for hd in heads]
        acc = [acc_scr[hd] for hd in heads]
        s = [_dot(ks[hd], qts[hd // 2]) + bias for hd in heads]
        m_new = [jnp.maximum(m_prev[hd], jnp.max(s[hd], axis=0, keepdims=True)) for hd in heads]
        p = [jnp.exp(s[hd] - m_new[hd]).astype(BF16) for hd in heads]
        acc = [acc[hd] * jnp.exp(m_prev[hd] - m_new[hd]) + _dot(vts[hd], p[hd]) for hd in heads]
        for hd in heads:
            acc_scr[hd] = acc[hd]
            m_scr[hd] = m_new[hd]
        return 0

    lax.fori_loop(0, i + 1, step, 0)
    for g in range(n_pairs):
        a0 = acc_scr[2 * g]
        a1 = acc_scr[2 * g + 1]
        num = jnp.where(top, a0, a1)
        den = jnp.concatenate([a0[HEAD_DIM:], a1[:HEAD_DIM]], axis=0)
        o_ref[:, g * LANES:(g + 1) * LANES] = (num / den).T.astype(BF16)


def _attention(proj, cos, sin_signed, batch, seq):
    t = ATTN_TILE
    nq = seq // t
    bias = _attn_bias_tiles(seq, t)
    gp = ATTN_PAIRS_PER_STEP
    width = gp * LANES
    n_groups = ATTN_WIDTH // width
    return pl.pallas_call(
        functools.partial(_attn_body, n_bias=bias.shape[0]),
        out_shape=jax.ShapeDtypeStruct((batch * seq, ATTN_WIDTH), BF16),
        grid=(batch, n_groups, nq),
        in_specs=[pl.BlockSpec((t, width), lambda b, p, i: (b * nq + i, p)),
                  pl.BlockSpec((seq, width), lambda b, p, i: (b, n_groups + p)),
                  pl.BlockSpec((seq, width), lambda b, p, i: (b, 2 * n_groups + p)),
                  pl.BlockSpec((t, LANES), lambda b, p, i: (i, 0)),
                  pl.BlockSpec((t, LANES), lambda b, p, i: (i, 0)),
                  _resident((seq, LANES)),
                  _resident((seq, LANES)),
                  _resident(bias.shape)],
        out_specs=pl.BlockSpec((t, width), lambda b, p, i: (b * nq + i, p)),
        scratch_shapes=[pltpu.VMEM((2 * gp, seq, LANES), BF16),
                        pltpu.VMEM((2 * gp, LANES, seq), BF16),
                        pltpu.VMEM((gp, LANES, t), BF16),
                        pltpu.VMEM((2 * gp, 1, t), F32),
                        pltpu.VMEM((2 * gp, LANES, t), F32)],
        compiler_params=_params(3),
        name="dilated_attention",
    )(proj, proj, proj, cos, sin_signed, cos, sin_signed, jnp.asarray(bias))


def _dot_nt(a, b):
    return lax.dot_general(a, b, (((1,), (1,)), ((), ())), preferred_element_type=F32)


def _dot_tn(a, b):
    return lax.dot_general(a, b, (((0,), (0,)), ((), ())), preferred_element_type=F32)


def _delta_body(q_ref, k_ref, v_ref, z_ref, gates_ref, cw_ref, alog_ref, dtb_ref, nw_ref, o_ref,
                lhs_scr, b_scr, u_scr, qk_scr, gl_scr, state_scr):
    C = DELTA_CHUNK
    H = N_DELTA_HEADS
    heads = range(H)
    n_chunks = q_ref.shape[0] // C
    halo = BF16_SUBLANES
    row = lax.broadcasted_iota(jnp.int32, (C, C), 0)
    col = lax.broadcasted_iota(jnp.int32, (C, C), 1)
    tril = row >= col
    strict = row > col
    eye = jnp.where(row == col, 1.0, 0.0)
    ones_tril = jnp.where(tril, 1.0, 0.0).astype(BF16)

    def load_inputs(n):
        r0 = pl.multiple_of(n * C, C)
        hs = pl.multiple_of(jnp.maximum(r0 - halo, 0), halo)
        xs = []
        for ref in (q_ref, k_ref, v_ref):
            per_head = []
            for h in heads:
                cs = slice(h * LANES, (h + 1) * LANES)
                xh = jnp.where(n > 0, ref[pl.ds(hs, halo), cs].astype(F32), 0.0)
                per_head.append(jnp.concatenate([xh, ref[pl.ds(r0, C), cs].astype(F32)], axis=0))
            xs.append(per_head)
        return gates_ref[pl.ds(r0, C), :], xs

    def conv_silu(xcat, part, h):
        wc = part * DELTA_WIDTH + h * LANES
        y = xcat[halo:] * cw_ref[DELTA_CONV - 1:DELTA_CONV, wc:wc + LANES]
        for s in range(1, DELTA_CONV):
            tap = DELTA_CONV - 1 - s
            y = y + pltpu.roll(xcat, s, axis=0)[halo:] * cw_ref[tap:tap + 1, wc:wc + LANES]
        return y * _sigmoid(y)

    def prepare(raw, xs):
        beta_all = _sigmoid(raw)
        sp_in = raw + dtb_ref[...]
        softplus = jnp.maximum(sp_in, 0.0) + jnp.log(1.0 + jnp.exp(-jnp.abs(sp_in)))
        g_step = -jnp.exp(alog_ref[...]) * softplus
        g_hi = g_step.astype(BF16)
        g_r = g_step - g_hi.astype(F32)
        g_mid = g_r.astype(BF16)
        g_lo = (g_r - g_mid.astype(F32)).astype(BF16)
        gsum = _dot(ones_tril, jnp.concatenate([g_hi, g_mid, g_lo], axis=1))
        gc = gsum[:, :LANES] + gsum[:, LANES:2 * LANES] + gsum[:, 2 * LANES:]
        gc_t = gc.T
        q = [conv_silu(xs[0][h], 0, h) for h in heads]
        k = [conv_silu(xs[1][h], 1, h) for h in heads]
        v = [conv_silu(xs[2][h], 2, h) for h in heads]
        q = [q[h] * lax.rsqrt(jnp.sum(q[h] * q[h], axis=-1, keepdims=True) + EPS)
             * (DELTA_DK ** -0.5) for h in heads]
        k = [k[h] * lax.rsqrt(jnp.sum(k[h] * k[h], axis=-1, keepdims=True) + EPS) for h in heads]
        gi = [gc[:, H + h:H + h + 1] for h in heads]
        gj = [gc_t[H + h:H + h + 1, :] for h in heads]
        g_last = [gc[C - 1:C, H + h:H + h + 1] for h in heads]
        beta = [beta_all[:, h:h + 1] for h in heads]
        decay = [jnp.exp(jnp.where(tril, gi[h] - gj[h], -jnp.inf)) for h in heads]
        kb = [k[h] * beta[h] for h in heads]
        kq = [_dot_nt(jnp.concatenate([kb[h], q[h]], axis=0).astype(BF16), k[h].astype(BF16))
              for h in heads]
        a = [jnp.where(strict, kq[h][:C] * decay[h], 0.0) for h in heads]
        qk = [kq[h][C:] * decay[h] for h in heads]
        x = [eye - a[h] for h in heads]
        p = a
        for _ in range(int(math.log2(C)) - 1):
            pb = [p[h].astype(BF16) for h in heads]
            p = [_dot(pb[h], pb[h]) for h in heads]
            x = [x[h] + _dot(x[h].astype(BF16), p[h].astype(BF16)) for h in heads]
        eg = [jnp.exp(gi[h]) for h in heads]
        uw = [_dot(x[h].astype(BF16),
                   jnp.concatenate([v[h] * beta[h], kb[h] * eg[h]], axis=1).astype(BF16))
              for h in heads]
        kd = [(k[h] * jnp.exp(g_last[h] - gi[h])).astype(BF16) for h in heads]
        gb = [_dot_tn(kd[h], jnp.concatenate([uw[h][:, DELTA_DV:], uw[h][:, :DELTA_DV]],
                                             axis=1).astype(BF16)) for h in heads]
        out = []
        for h in heads:
            u, w = uw[h][:, :DELTA_DV], uw[h][:, DELTA_DV:]
            lhs = jnp.concatenate([gb[h][:, :DELTA_DK], w, q[h] * eg[h]], axis=0).astype(BF16)
            gl = jnp.broadcast_to(jnp.exp(g_last[h]), gl_scr.shape[2:])
            out.append((lhs, gb[h][:, DELTA_DK:], u, qk[h].astype(BF16), gl))
        return out

    def store_prepared(slot, prepared):
        for h in heads:
            lhs, bmat, u, qk, gl = prepared[h]
            lhs_scr[slot, h] = lhs
            b_scr[slot, h] = bmat
            u_scr[slot, h] = u
            qk_scr[slot, h] = qk
            gl_scr[slot, h] = gl

    def load_prepared(n, slot):
        r0 = pl.multiple_of(n * C, C)
        return [(lhs_scr[slot, h], b_scr[slot, h], u_scr[slot, h], qk_scr[slot, h],
                 gl_scr[slot, h][0:1, :], state_scr[h],
                 z_ref[pl.ds(r0, C), h * LANES:(h + 1) * LANES].astype(F32)) for h in heads]

    def consume(loaded):
        r = [_dot(loaded[h][0], loaded[h][5].astype(BF16)) for h in heads]
        v_new = [loaded[h][2] - r[h][DELTA_DK:DELTA_DK + C] for h in heads]
        o = [r[h][DELTA_DK + C:] + _dot(loaded[h][3], v_new[h].astype(BF16)) for h in heads]
        s_new = [loaded[h][4] * loaded[h][5] + loaded[h][1] - r[h][:DELTA_DK] for h in heads]
        outs = []
        for h in heads:
            z = loaded[h][6]
            outs.append((_rms(o[h], nw_ref[...]) * z * _sigmoid(z)).astype(BF16))
        return s_new, outs

    state_scr[...] = jnp.zeros(state_scr.shape, F32)
    store_prepared(0, prepare(*load_inputs(0)))

    def step(n, _):
        slot = n % 2
        r0 = pl.multiple_of(n * C, C)
        loaded = load_prepared(n, slot)
        raw, xs = load_inputs(jnp.minimum(n + 1, n_chunks - 1))
        s_new, outs = consume(loaded)
        prepared = prepare(raw, xs)
        for h in heads:
            state_scr[h] = s_new[h]
            o_ref[pl.ds(r0, C), h * LANES:(h + 1) * LANES] = outs[h]
        store_prepared(1 - slot, prepared)
        return 0

    lax.fori_loop(0, n_chunks, step, 0)


def _delta(proj, gates, conv_w, alog, dtb, norm_w, batch, seq):
    C = DELTA_CHUNK
    H = N_DELTA_HEADS
    first = 3 * ATTN_WIDTH // DELTA_WIDTH
    blk = lambda j: pl.BlockSpec((seq, DELTA_WIDTH), lambda b: (b, first + j))
    return pl.pallas_call(
        _delta_body,
        out_shape=jax.ShapeDtypeStruct((batch * seq, DELTA_WIDTH), BF16),
        grid=(batch,),
        in_specs=[blk(0), blk(1), blk(2), blk(3),
                  pl.BlockSpec((seq, LANES), lambda b: (b, 0)),
                  _resident(conv_w.shape), _resident((1, LANES)), _resident((1, LANES)),
                  _resident((1, DELTA_DV))],
        out_specs=pl.BlockSpec((seq, DELTA_WIDTH), lambda b: (b, 0)),
        scratch_shapes=[pltpu.VMEM((2, H, DELTA_DK + 2 * C, DELTA_DK), BF16),
                        pltpu.VMEM((2, H, DELTA_DK, DELTA_DV), F32),
                        pltpu.VMEM((2, H, C, DELTA_DV), F32),
                        pltpu.VMEM((2, H, C, C), BF16),
                        pltpu.VMEM((2, H, 8, DELTA_DV), F32),
                        pltpu.VMEM((H, DELTA_DK, DELTA_DV), F32)],
        compiler_params=_params(1),
        name="gated_delta",
    )(proj, proj, proj, proj, gates, conv_w, alog, dtb, norm_w)


def _outproj_body(a_ref, d_ref, x_ref, w_ref, nw_ref, o_ref):
    wa = a_ref.shape[1]
    mix = _dot(a_ref[...], w_ref[:wa, :]) + _dot(d_ref[...], w_ref[wa:, :])
    o_ref[...] = x_ref[...] + _rms(mix, nw_ref[...])


def _outproj(attn, dn, x, w, nw):
    tok, d = x.shape
    return pl.pallas_call(
        _outproj_body,
        out_shape=jax.ShapeDtypeStruct((tok, d), F32),
        grid=(tok // ROW_TILE,),
        in_specs=[pl.BlockSpec((ROW_TILE, attn.shape[1]), lambda i: (i, 0)),
                  pl.BlockSpec((ROW_TILE, dn.shape[1]), lambda i: (i, 0)),
                  pl.BlockSpec((ROW_TILE, d), lambda i: (i, 0)),
                  _resident(w.shape), _resident((1, d))],
        out_specs=pl.BlockSpec((ROW_TILE, d), lambda i: (i, 0)),
        compiler_params=_params(1),
        name="outproj",
    )(attn, dn, x, w, nw)


def _gelu_tanh(x):
    return 0.5 * x * (1.0 + jnp.tanh(math.sqrt(2.0 / math.pi) * (x + 0.044715 * (x * x * x))))


def _ffn_body(x_ref, nw_pre_ref, w_in_ref, cw_ref, cb_ref, w_out_ref, nw_post_ref, o_ref,
              h_scr, acc_scr, carry_scr, *, tiles_per_seq):
    tm = x_ref.shape[0]
    d_ff = w_out_ref.shape[0]
    tf = FFN_COL_TILE
    x = x_ref[...]
    h_scr[...] = _rms(x, nw_pre_ref[...]).astype(BF16)
    acc_scr[...] = jnp.zeros(acc_scr.shape, F32)
    seq_start = pl.program_id(0) % tiles_per_seq == 0

    def conv_branch(col):
        cs = pl.ds(col, tf)
        u = _dot(h_scr[...], w_in_ref[:, cs])
        ucat = jnp.concatenate([jnp.where(seq_start, 0.0, carry_scr[:, cs]), u], axis=0)
        carry_scr[:, cs] = u[tm - 8:, :]
        y = u * cw_ref[FFN_CONV - 1:FFN_CONV, cs] + cb_ref[:, cs]
        for s in range(1, FFN_CONV):
            tap = FFN_CONV - 1 - s
            y = y + pltpu.roll(ucat, s, axis=0)[8:] * cw_ref[tap:tap + 1, cs]
        return y

    def chunk(c, _):
        col = pl.multiple_of(c * tf, tf)
        gate = conv_branch(col)
        up = conv_branch(pl.multiple_of(d_ff + col, tf))
        act = (_gelu_tanh(gate) * up).astype(BF16)
        acc_scr[...] += _dot(act, w_out_ref[pl.ds(col, tf), :])
        return 0

    lax.fori_loop(0, d_ff // tf, chunk, 0)
    o_ref[...] = x + _rms(acc_scr[...], nw_post_ref[...])


def _ffn(x, nw_pre, w_in, conv_w, conv_b, w_out, nw_post, seq):
    tok, d = x.shape
    d_ff = w_out.shape[0]
    assert d_ff % FFN_COL_TILE == 0 and seq % ROW_TILE == 0
    return pl.pallas_call(
        functools.partial(_ffn_body, tiles_per_seq=seq // ROW_TILE),
        out_shape=jax.ShapeDtypeStruct((tok, d), F32),
        grid=(tok // ROW_TILE,),
        in_specs=[pl.BlockSpec((ROW_TILE, d), lambda i: (i, 0)),
                  _resident((1, d)), _resident(w_in.shape), _resident(conv_w.shape),
                  _resident((1, 2 * d_ff)), _resident(w_out.shape), _resident((1, d))],
        out_specs=pl.BlockSpec((ROW_TILE, d), lambda i: (i, 0)),
        scratch_shapes=[pltpu.VMEM((ROW_TILE, d), BF16),
                        pltpu.VMEM((ROW_TILE, d), F32),
                        pltpu.VMEM((8, 2 * d_ff), F32)],
        compiler_params=_params(1),
        name="conv_glu_ffn",
    )(x, nw_pre, w_in, conv_w, conv_b, w_out, nw_post)


def _rope_tables(seq):
    half = HEAD_DIM // 2
    inv = 1.0 / (ROPE_THETA ** (jnp.arange(0, HEAD_DIM, 2, dtype=F32) / HEAD_DIM))
    ang = jnp.arange(seq, dtype=F32)[:, None] * inv[None, :]
    cos, sin = jnp.cos(ang), jnp.sin(ang)
    reps = LANES // half
    sign = jnp.tile(jnp.concatenate([-jnp.ones((half,), F32), jnp.ones((half,), F32)]), reps // 2)
    return jnp.tile(cos, (1, reps)), jnp.tile(sin, (1, reps)) * sign[None, :]


def _lane_row(vals, offset):
    return jnp.zeros((1, LANES), F32).at[0, offset:offset + vals.shape[0]].set(vals.astype(F32))


def kernel(x, w_in, dn_conv_w, dn_a_log, dn_dt_bias, dn_norm_w, w_out, ffn_w_in, ffn_conv_w,
           ffn_conv_b, ffn_w_out, norm_pre_mix, norm_post_mix, norm_pre_ffn, norm_post_ffn):
    batch, seq, d = x.shape
    depth = w_in.shape[0]
    assert w_in.shape[2] == MAIN_COLS + GATE_COLS
    cos, sin_signed = _rope_tables(seq)
    xt = x.reshape(batch * seq, d)
    for l in range(depth):
        w_main = w_in[l, :, :MAIN_COLS].astype(BF16)
        w_gate = jnp.pad(w_in[l, :, MAIN_COLS:], ((0, 0), (0, LANES - GATE_COLS))).astype(BF16)
        proj, gates = _inproj(xt, norm_pre_mix[l][None, :], w_main, w_gate)
        attn = _attention(proj, cos, sin_signed, batch, seq)
        dn = _delta(proj, gates, dn_conv_w[l],
                    _lane_row(dn_a_log[l], N_DELTA_HEADS), _lane_row(dn_dt_bias[l], N_DELTA_HEADS),
                    dn_norm_w[l][None, :], batch, seq)
        xt = _outproj(attn, dn, xt, w_out[l].astype(BF16), norm_post_mix[l][None, :])
        xt = _ffn(xt, norm_pre_ffn[l][None, :], ffn_w_in[l].astype(BF16), ffn_conv_w[l],
                  ffn_conv_b[l][None, :], ffn_w_out[l].astype(BF16), norm_post_ffn[l][None, :], seq)
    return xt.reshape(batch, seq, d)
```

```python
import functools
import math

import numpy as np
import jax
import jax.numpy as jnp
from jax import lax
from jax.experimental import pallas as pl
from jax.experimental.pallas import tpu as pltpu

F32 = jnp.float32
BF16 = jnp.bfloat16

HEAD_DIM = 64
N_ATTN_HEADS = 8
ATTN_WIDTH = N_ATTN_HEADS * HEAD_DIM
DILATED_PATTERNS = ((128, 1), (512, 4), (2048, 16))
ROPE_THETA = 10000.0
MASK_VALUE = -1e30
LOG2_E = math.log2(math.e)

N_DELTA_HEADS = 4
DELTA_DK = 128
DELTA_DV = 128
DELTA_WIDTH = N_DELTA_HEADS * DELTA_DK
DELTA_CONV = 4
DELTA_CHUNK = 64

MAIN_COLS = 3 * ATTN_WIDTH + 4 * DELTA_WIDTH
GATE_COLS = 2 * N_DELTA_HEADS
FFN_CONV = 3
EPS = 1e-6

LANES = 128
SUBLANES = 8
VMEM_LIMIT = 56 * 1024 * 1024

ROW_TILE = 512
FFN_COL_TILE = 256
FFN_ROW_BLOCKS = 4
ATTN_TILE = 512
ATTN_PAIRS_PER_STEP = 2
DELTA_GROUP = 4


def _resident(shape):
    nd = len(shape)
    return pl.BlockSpec(shape, lambda *_: (0,) * nd, pipeline_mode=pl.Buffered(1))


def _params(n_grid, flags=None):
    return pltpu.CompilerParams(dimension_semantics=("arbitrary",) * n_grid,
                                vmem_limit_bytes=VMEM_LIMIT, flags=flags)


def _rms(x, w):
    return x * lax.rsqrt(jnp.mean(x * x, axis=-1, keepdims=True) + EPS) * w


def _sigmoid(x):
    return 1.0 / (1.0 + jnp.exp(-x))


def _dot(a, b):
    return jnp.dot(a, b, preferred_element_type=F32)


def _causal_taps(xcat, x, w_ref, cols, n_taps):
    y = x * w_ref[n_taps - 1:n_taps, cols]
    for s in range(1, n_taps):
        tap = n_taps - 1 - s
        y = y + pltpu.roll(xcat, s, axis=0)[SUBLANES:] * w_ref[tap:tap + 1, cols]
    return y


def _inproj_body(x_ref, nw_ref, w_ref, wg_ref, cw_ref, alog_ref, dtb_ref, proj_ref, gates_ref,
                 carry_scr, *, tiles_per_seq):
    tm = x_ref.shape[0]
    tn = DELTA_WIDTH
    h = _rms(x_ref[...], nw_ref[...]).astype(BF16)
    seq_start = pl.program_id(0) % tiles_per_seq == 0
    first_delta = 3 * ATTN_WIDTH // tn
    carries = [jnp.where(seq_start, 0.0, carry_scr[part]) for part in range(3)]
    new_carries = []
    n_slabs = proj_ref.shape[1] // tn
    plain = [c for c in range(n_slabs) if not first_delta <= c < first_delta + 3]
    order = [c for pair in zip(range(first_delta, first_delta + 3), plain) for c in pair] + plain[3:]
    for c in order:
        cs = slice(c * tn, (c + 1) * tn)
        y = _dot(h, w_ref[:, cs])
        part = c - first_delta
        if 0 <= part < 3:
            ycat = jnp.concatenate([carries[part], y], axis=0)
            new_carries.append(y[tm - SUBLANES:])
            z = _causal_taps(ycat, y, cw_ref, slice(part * tn, (part + 1) * tn), DELTA_CONV)
            z = z * _sigmoid(z)
            if part < 2:
                scale = DELTA_DK ** -0.5 if part == 0 else 1.0
                segs = []
                for hh in range(N_DELTA_HEADS):
                    seg = z[:, hh * DELTA_DK:(hh + 1) * DELTA_DK]
                    inv = lax.rsqrt(jnp.sum(seg * seg, axis=-1, keepdims=True) + EPS)
                    segs.append(seg * (inv * scale))
                z = jnp.concatenate(segs, axis=1)
            y = z
        proj_ref[:, cs] = y.astype(BF16)
    g = _dot(h, wg_ref[...])
    sp_in = g + dtb_ref[...]
    softplus = jnp.maximum(sp_in, 0.0) + jnp.log(1.0 + jnp.exp(-jnp.abs(sp_in)))
    lane = lax.broadcasted_iota(jnp.int32, (1, LANES), 1)
    gates_ref[...] = jnp.where(lane < N_DELTA_HEADS, _sigmoid(g), -jnp.exp(alog_ref[...]) * softplus)
    for part in range(3):
        carry_scr[part] = new_carries[part]


def _inproj(x, nw, w_main, w_gate, conv_w, alog, dtb, seq):
    tok, d = x.shape
    assert seq % ROW_TILE == 0
    return pl.pallas_call(
        functools.partial(_inproj_body, tiles_per_seq=seq // ROW_TILE),
        out_shape=(jax.ShapeDtypeStruct((tok, MAIN_COLS), BF16),
                   jax.ShapeDtypeStruct((tok, LANES), F32)),
        grid=(tok // ROW_TILE,),
        in_specs=[pl.BlockSpec((ROW_TILE, d), lambda i: (i, 0)),
                  _resident((1, d)),
                  _resident((d, MAIN_COLS)),
                  _resident((d, LANES)),
                  _resident(conv_w.shape), _resident((1, LANES)), _resident((1, LANES))],
        out_specs=(pl.BlockSpec((ROW_TILE, MAIN_COLS), lambda i: (i, 0)),
                   pl.BlockSpec((ROW_TILE, LANES), lambda i: (i, 0))),
        scratch_shapes=[pltpu.VMEM((3, SUBLANES, DELTA_WIDTH), F32)],
        compiler_params=_params(1),
        name="inproj",
    )(x, nw, w_main, w_gate, conv_w, alog, dtb)


def _attn_bias_tiles(seq, t):
    a = np.arange(t)[:, None]
    c = np.arange(t)[None, :]
    tiles = []
    for d in range(seq // t):
        delta = d * t + a - c
        mult = np.zeros((t, t), np.int64)
        for window, dil in DILATED_PATTERNS:
            mult += (delta >= 0) & (delta % dil == 0) & (delta <= window)
        tiles.append(np.where(mult > 0, np.log2(np.maximum(mult, 1)), MASK_VALUE).astype(np.float32))
    n_distinct = 1
    while not all(np.array_equal(tiles[n_distinct - 1], tl) for tl in tiles[n_distinct:]):
        n_distinct += 1
    return np.stack(tiles[:n_distinct]).transpose(0, 2, 1)


def _rope(x, cos, sin_signed):
    lane = lax.broadcasted_iota(jnp.int32, (1, LANES), 1)
    partner = jnp.where((lane % HEAD_DIM) < HEAD_DIM // 2,
                        pltpu.roll(x, LANES - HEAD_DIM // 2, axis=1),
                        pltpu.roll(x, HEAD_DIM // 2, axis=1))
    return x * cos + partner * sin_signed


def _attn_body(q_ref, k_ref, v_ref, cq_ref, sq_ref, ck_ref, sk_ref, bias_ref, o_ref,
               k_scr, vt_scr, qt_scr, m_scr, acc_scr, *, n_bias):
    t = ATTN_TILE
    n_pairs = qt_scr.shape[0]
    i = pl.program_id(2)
    head0 = lax.broadcasted_iota(jnp.int32, (1, LANES), 1) < HEAD_DIM
    top = lax.broadcasted_iota(jnp.int32, (LANES, 1), 0) < HEAD_DIM

    @pl.when(i == 0)
    def _():
        for g in range(n_pairs):
            cs = slice(g * LANES, (g + 1) * LANES)
            for c in range(k_ref.shape[0] // t):
                rs = slice(c * t, (c + 1) * t)
                k = _rope(k_ref[rs, cs].astype(F32), ck_ref[rs, :], sk_ref[rs, :])
                k_scr[2 * g, rs, :] = jnp.where(head0, k, 0.0).astype(BF16)
                k_scr[2 * g + 1, rs, :] = jnp.where(head0, 0.0, k).astype(BF16)
                vt = v_ref[rs, cs].astype(F32).T
                vt_scr[2 * g, :, rs] = jnp.where(top, vt, 1.0).astype(BF16)
                vt_scr[2 * g + 1, :, rs] = jnp.where(top, 1.0, vt).astype(BF16)

    for g in range(n_pairs):
        cs = slice(g * LANES, (g + 1) * LANES)
        q = _rope(q_ref[:, cs].astype(F32), cq_ref[...], sq_ref[...]) * (HEAD_DIM ** -0.5 * LOG2_E)
        qt_scr[g] = q.T.astype(BF16)
    m_scr[...] = jnp.full(m_scr.shape, MASK_VALUE, F32)
    acc_scr[...] = jnp.zeros(acc_scr.shape, F32)

    def step(j, _):
        bias = bias_ref[jnp.minimum(i - j, n_bias - 1)]
        col = pl.multiple_of(j * t, t)
        heads = range(2 * n_pairs)
        ks = [k_scr[hd, pl.ds(col, t), :] for hd in heads]
        vts = [vt_scr[hd, :, pl.ds(col, t)] for hd in heads]
        qts = [qt_scr[g] for g in range(n_pairs)]
        m_prev = [m_scr[hd] for hd in heads]
        acc = [acc_scr[hd] for hd in heads]
        s = [_dot(ks[hd], qts[hd // 2]) + bias for hd in heads]
        m_new = [jnp.maximum(m_prev[hd], jnp.max(s[hd], axis=0, keepdims=True)) for hd in heads]
        p = [jnp.exp2(s[hd] - m_new[hd]).astype(BF16) for hd in heads]
        acc = [acc[hd] * jnp.exp2(m_prev[hd] - m_new[hd]) + _dot(vts[hd], p[hd]) for hd in heads]
        for hd in heads:
            acc_scr[hd] = acc[hd]
            m_scr[hd] = m_new[hd]
        return 0

    lax.fori_loop(0, i + 1, step, 0)
    for g in range(n_pairs):
        a0 = acc_scr[2 * g]
        a1 = acc_scr[2 * g + 1]
        num = jnp.where(top, a0, a1)
        den = jnp.concatenate([a0[HEAD_DIM:], a1[:HEAD_DIM]], axis=0)
        o_ref[:, g * LANES:(g + 1) * LANES] = (num / den).T.astype(BF16)


def _attention(proj, cos, sin_signed, batch, seq):
    t = ATTN_TILE
    nq = seq // t
    bias = _attn_bias_tiles(seq, t)
    gp = ATTN_PAIRS_PER_STEP
    width = gp * LANES
    n_groups = ATTN_WIDTH // width
    return pl.pallas_call(
        functools.partial(_attn_body, n_bias=bias.shape[0]),
        out_shape=jax.ShapeDtypeStruct((batch * seq, ATTN_WIDTH), BF16),
        grid=(batch, n_groups, nq),
        in_specs=[pl.BlockSpec((t, width), lambda b, p, i: (b * nq + i, p)),
                  pl.BlockSpec((seq, width), lambda b, p, i: (b, n_groups + p)),
                  pl.BlockSpec((seq, width), lambda b, p, i: (b, 2 * n_groups + p)),
                  pl.BlockSpec((t, LANES), lambda b, p, i: (i, 0)),
                  pl.BlockSpec((t, LANES), lambda b, p, i: (i, 0)),
                  _resident((seq, LANES)),
                  _resident((seq, LANES)),
                  _resident(bias.shape)],
        out_specs=pl.BlockSpec((t, width), lambda b, p, i: (b * nq + i, p)),
        scratch_shapes=[pltpu.VMEM((2 * gp, seq, LANES), BF16),
                        pltpu.VMEM((2 * gp, LANES, seq), BF16),
                        pltpu.VMEM((gp, LANES, t), BF16),
                        pltpu.VMEM((2 * gp, 1, t), F32),
                        pltpu.VMEM((2 * gp, LANES, t), F32)],
        compiler_params=_params(3),
        name="dilated_attention",
    )(proj, proj, proj, cos, sin_signed, cos, sin_signed, jnp.asarray(bias))


def _dot_nt(a, b):
    return lax.dot_general(a, b, (((1,), (1,)), ((), ())), preferred_element_type=F32)


def _dot_tn(a, b):
    return lax.dot_general(a, b, (((0,), (0,)), ((), ())), preferred_element_type=F32)


def _delta_body(q_ref, k_ref, v_ref, z_ref, gates_ref, nw_ref, o_ref,
                lhs_scr, b_scr, u_scr, qk_scr, gl_scr, state_scr):
    C = DELTA_CHUNK
    H = N_DELTA_HEADS
    G = DELTA_GROUP
    heads = range(H)
    items = [(j, h) for j in range(G) for h in heads]
    n_groups = q_ref.shape[0] // (C * G)
    row = lax.broadcasted_iota(jnp.int32, (C, C), 0)
    col = lax.broadcasted_iota(jnp.int32, (C, C), 1)
    tril = row >= col
    strict = row > col
    eye = jnp.where(row == col, 1.0, 0.0)
    ones_tril = jnp.where(tril, 1.0, 0.0).astype(BF16)

    def rows(grp, j):
        return pl.ds(pl.multiple_of((grp * G + j) * C, C), C)

    def lanes(h):
        return slice(h * LANES, (h + 1) * LANES)

    def load_inputs(grp):
        gates = [gates_ref[rows(grp, j), :] for j in range(G)]
        qkv = [{(j, h): ref[rows(grp, j), lanes(h)].astype(F32) for (j, h) in items}
               for ref in (q_ref, k_ref, v_ref)]
        return gates, qkv

    def prepare(gates, qkv):
        q, k, v = qkv
        gc, gc_t = [], []
        for j in range(G):
            g_step = gates[j]
            g_hi = g_step.astype(BF16)
            g_r = g_step - g_hi.astype(F32)
            g_mid = g_r.astype(BF16)
            g_lo = (g_r - g_mid.astype(F32)).astype(BF16)
            gsum = _dot(ones_tril, jnp.concatenate([g_hi, g_mid, g_lo], axis=1))
            gc.append(gsum[:, :LANES] + gsum[:, LANES:2 * LANES] + gsum[:, 2 * LANES:])
            gc_t.append(gc[j].T)
        gi = {(j, h): gc[j][:, H + h:H + h + 1] for (j, h) in items}
        gj = {(j, h): gc_t[j][H + h:H + h + 1, :] for (j, h) in items}
        g_last = {(j, h): gc[j][C - 1:C, H + h:H + h + 1] for (j, h) in items}
        beta = {(j, h): gates[j][:, h:h + 1] for (j, h) in items}
        decay = {it: jnp.exp(jnp.where(tril, gi[it] - gj[it], -jnp.inf)) for it in items}
        kb = {it: k[it] * beta[it] for it in items}
        kq = {it: _dot_nt(jnp.concatenate([kb[it], q[it]], axis=0).astype(BF16),
                          k[it].astype(BF16)) for it in items}
        a = {it: jnp.where(strict, kq[it][:C] * decay[it], 0.0) for it in items}
        qk = {it: kq[it][C:] * decay[it] for it in items}
        x = {it: eye - a[it] for it in items}
        p = a
        for _ in range(int(math.log2(C)) - 1):
            pb = {it: p[it].astype(BF16) for it in items}
            p = {it: _dot(pb[it], pb[it]) for it in items}
            x = {it: x[it] + _dot(x[it].astype(BF16), p[it].astype(BF16)) for it in items}
        eg = {it: jnp.exp(gi[it]) for it in items}
        uw = {it: _dot(x[it].astype(BF16),
                       jnp.concatenate([v[it] * beta[it], kb[it] * eg[it]], axis=1).astype(BF16))
              for it in items}
        kd = {it: (k[it] * jnp.exp(g_last[it] - gi[it])).astype(BF16) for it in items}
        gb = {it: _dot_tn(kd[it], jnp.concatenate([uw[it][:, DELTA_DV:], uw[it][:, :DELTA_DV]],
                                                  axis=1).astype(BF16)) for it in items}
        out = {}
        for it in items:
            u, w = uw[it][:, :DELTA_DV], uw[it][:, DELTA_DV:]
            lhs = jnp.concatenate([gb[it][:, :DELTA_DK], w, q[it] * eg[it]], axis=0).astype(BF16)
            gl = jnp.broadcast_to(jnp.exp(g_last[it]), gl_scr.shape[3:])
            out[it] = (lhs, gb[it][:, DELTA_DK:], u, qk[it].astype(BF16), gl)
        return out

    def store_prepared(slot, prepared):
        for (j, h) in items:
            lhs, bmat, u, qk, gl = prepared[(j, h)]
            lhs_scr[slot, j, h] = lhs
            b_scr[slot, j, h] = bmat
            u_scr[slot, j, h] = u
            qk_scr[slot, j, h] = qk
            gl_scr[slot, j, h] = gl

    def load_prepared(grp, slot):
        prepared = {(j, h): (lhs_scr[slot, j, h], b_scr[slot, j, h], u_scr[slot, j, h],
                             qk_scr[slot, j, h], gl_scr[slot, j, h][0:1, :]) for (j, h) in items}
        z = {(j, h): z_ref[rows(grp, j), lanes(h)].astype(F32) for (j, h) in items}
        return prepared, z, [state_scr[h] for h in heads]

    def consume(prepared, z, state):
        outs = {}
        for j in range(G):
            r = [_dot(prepared[(j, h)][0], state[h].astype(BF16)) for h in heads]
            v_new = [prepared[(j, h)][2] - r[h][DELTA_DK:DELTA_DK + C] for h in heads]
            o = [r[h][DELTA_DK + C:] + _dot(prepared[(j, h)][3], v_new[h].astype(BF16))
                 for h in heads]
            state = [prepared[(j, h)][4] * state[h] + prepared[(j, h)][1] - r[h][:DELTA_DK]
                     for h in heads]
            for h in heads:
                zz = z[(j, h)]
                outs[(j, h)] = (_rms(o[h], nw_ref[...]) * zz * _sigmoid(zz)).astype(BF16)
        return state, outs

    state_scr[...] = jnp.zeros(state_scr.shape, F32)
    store_prepared(0, prepare(*load_inputs(0)))

    def step(grp, _):
        slot = grp % 2
        prepared, z, state = load_prepared(grp, slot)
        gates, qkv = load_inputs(jnp.minimum(grp + 1, n_groups - 1))
        state, outs = consume(prepared, z, state)
        nxt = prepare(gates, qkv)
        for h in heads:
            state_scr[h] = state[h]
        for (j, h) in items:
            o_ref[rows(grp, j), lanes(h)] = outs[(j, h)]
        store_prepared(1 - slot, nxt)
        return 0

    lax.fori_loop(0, n_groups, step, 0)


def _delta(proj, gates, norm_w, batch, seq):
    C = DELTA_CHUNK
    H = N_DELTA_HEADS
    G = DELTA_GROUP
    assert seq % (C * G) == 0
    first = 3 * ATTN_WIDTH // DELTA_WIDTH
    blk = lambda j: pl.BlockSpec((seq, DELTA_WIDTH), lambda b: (b, first + j))
    return pl.pallas_call(
        _delta_body,
        out_shape=jax.ShapeDtypeStruct((batch * seq, DELTA_WIDTH), BF16),
        grid=(batch,),
        in_specs=[blk(0), blk(1), blk(2), blk(3),
                  pl.BlockSpec((seq, LANES), lambda b: (b, 0)),
                  _resident((1, DELTA_DV))],
        out_specs=pl.BlockSpec((seq, DELTA_WIDTH), lambda b: (b, 0)),
        scratch_shapes=[pltpu.VMEM((2, G, H, DELTA_DK + 2 * C, DELTA_DK), BF16),
                        pltpu.VMEM((2, G, H, DELTA_DK, DELTA_DV), F32),
                        pltpu.VMEM((2, G, H, C, DELTA_DV), F32),
                        pltpu.VMEM((2, G, H, C, C), BF16),
                        pltpu.VMEM((2, G, H, SUBLANES, DELTA_DV), F32),
                        pltpu.VMEM((H, DELTA_DK, DELTA_DV), F32)],
        compiler_params=_params(1),
        name="gated_delta",
    )(proj, proj, proj, proj, gates, norm_w)


def _outproj_body(a_ref, d_ref, x_ref, w_ref, nw_ref, o_ref):
    wa = a_ref.shape[1]
    mix = _dot(a_ref[...], w_ref[:wa, :]) + _dot(d_ref[...], w_ref[wa:, :])
    o_ref[...] = x_ref[...] + _rms(mix, nw_ref[...])


def _outproj(attn, dn, x, w, nw):
    tok, d = x.shape
    return pl.pallas_call(
        _outproj_body,
        out_shape=jax.ShapeDtypeStruct((tok, d), F32),
        grid=(tok // ROW_TILE,),
        in_specs=[pl.BlockSpec((ROW_TILE, attn.shape[1]), lambda i: (i, 0)),
                  pl.BlockSpec((ROW_TILE, dn.shape[1]), lambda i: (i, 0)),
                  pl.BlockSpec((ROW_TILE, d), lambda i: (i, 0)),
                  _resident(w.shape), _resident((1, d))],
        out_specs=pl.BlockSpec((ROW_TILE, d), lambda i: (i, 0)),
        compiler_params=_params(1),
        name="outproj",
    )(attn, dn, x, w, nw)


def _gelu_tanh(x):
    a = -2.0 * math.sqrt(2.0 / math.pi) * math.log2(math.e)
    b = 0.044715 * a
    return x / (1.0 + jnp.exp2(x * (a + b * (x * x))))


def _ffn_body(x_ref, nw_pre_ref, w_in_ref, cw_ref, cb_ref, w_out_ref, nw_post_ref, o_ref,
              h_scr, acc_scr, carry_scr, u_scr, *, tiles_per_seq):
    tm = x_ref.shape[0]
    d_ff = w_out_ref.shape[0]
    tf = FFN_COL_TILE
    n_chunks = d_ff // tf
    x = x_ref[...]
    h_scr[...] = _rms(x, nw_pre_ref[...]).astype(BF16)
    seq_start = pl.program_id(0) % tiles_per_seq == 0

    def cols(c, branch):
        return pl.ds(pl.multiple_of(branch * d_ff + c * tf, tf), tf)

    tb = tm // FFN_ROW_BLOCKS

    def up_proj(c, r):
        h = h_scr[r * tb:(r + 1) * tb, :]
        return [_dot(h, w_in_ref[:, cols(c, br)]) for br in range(2)]

    def prev_rows(c):
        return [jnp.where(seq_start, 0.0, carry_scr[:, cols(c, br)]) for br in range(2)]

    def store_u(c_prev_rows, blocks):
        for br in range(2):
            u_scr[br, :SUBLANES, :] = c_prev_rows[br]
            for r in range(FFN_ROW_BLOCKS):
                u_scr[br, SUBLANES + r * tb:SUBLANES + (r + 1) * tb, :] = blocks[r][br]

    def activation(c, r):
        y = []
        for br in range(2):
            cs = cols(c, br)
            ucat = u_scr[br, r * tb:SUBLANES + (r + 1) * tb, :]
            y.append(_causal_taps(ucat, ucat[SUBLANES:], cw_ref, cs, FFN_CONV) + cb_ref[:, cs])
        return (_gelu_tanh(y[0]) * y[1]).astype(BF16)

    def run_chunk(c, first, has_next):
        w_down = w_out_ref[pl.ds(pl.multiple_of(c * tf, tf), tf), :]
        last_rows = [u_scr[br, tm:, :] for br in range(2)]
        nxt_prev = prev_rows(c + 1) if has_next else None
        acc, nxt = [], []
        for r in range(FFN_ROW_BLOCKS):
            if has_next:
                nxt.append(up_proj(c + 1, r))
            down = _dot(activation(c, r), w_down)
            acc.append(down if first else acc_scr[r * tb:(r + 1) * tb, :] + down)
        for br in range(2):
            carry_scr[:, cols(c, br)] = last_rows[br]
        if has_next:
            store_u(nxt_prev, nxt)
        return acc

    def store_acc(acc):
        for r in range(FFN_ROW_BLOCKS):
            acc_scr[r * tb:(r + 1) * tb, :] = acc[r]

    store_u(prev_rows(0), [up_proj(0, r) for r in range(FFN_ROW_BLOCKS)])
    store_acc(run_chunk(0, True, True))

    def step(c, _):
        store_acc(run_chunk(c, False, True))
        return 0

    lax.fori_loop(1, n_chunks - 1, step, 0)
    f = jnp.concatenate(run_chunk(n_chunks - 1, False, False), axis=0)
    o_ref[...] = x + _rms(f, nw_post_ref[...])


def _ffn(x, nw_pre, w_in, conv_w, conv_b, w_out, nw_post, seq):
    tok, d = x.shape
    d_ff = w_out.shape[0]
    assert d_ff % FFN_COL_TILE == 0 and d_ff // FFN_COL_TILE >= 3 and seq % ROW_TILE == 0
    return pl.pallas_call(
        functools.partial(_ffn_body, tiles_per_seq=seq // ROW_TILE),
        out_shape=jax.ShapeDtypeStruct((tok, d), F32),
        grid=(tok // ROW_TILE,),
        in_specs=[pl.BlockSpec((ROW_TILE, d), lambda i: (i, 0)),
                  _resident((1, d)), _resident(w_in.shape), _resident(conv_w.shape),
                  _resident((1, 2 * d_ff)), _resident(w_out.shape), _resident((1, d))],
        out_specs=pl.BlockSpec((ROW_TILE, d), lambda i: (i, 0)),
        scratch_shapes=[pltpu.VMEM((ROW_TILE, d), BF16),
                        pltpu.VMEM((ROW_TILE, d), F32),
                        pltpu.VMEM((SUBLANES, 2 * d_ff), F32),
                        pltpu.VMEM((2, SUBLANES + ROW_TILE, FFN_COL_TILE), F32)],
        compiler_params=_params(1),
        name="conv_glu_ffn",
    )(x, nw_pre, w_in, conv_w, conv_b, w_out, nw_post)


def _rope_tables(seq):
    half = HEAD_DIM // 2
    inv = 1.0 / (ROPE_THETA ** (jnp.arange(0, HEAD_DIM, 2, dtype=F32) / HEAD_DIM))
    ang = jnp.arange(seq, dtype=F32)[:, None] * inv[None, :]
    cos, sin = jnp.cos(ang), jnp.sin(ang)
    reps = LANES // half
    sign = jnp.tile(jnp.concatenate([-jnp.ones((half,), F32), jnp.ones((half,), F32)]), reps // 2)
    return jnp.tile(cos, (1, reps)), jnp.tile(sin, (1, reps)) * sign[None, :]


def _lane_row(vals, offset):
    return jnp.zeros((1, LANES), F32).at[0, offset:offset + vals.shape[0]].set(vals.astype(F32))


def kernel(x, w_in, dn_conv_w, dn_a_log, dn_dt_bias, dn_norm_w, w_out, ffn_w_in, ffn_conv_w,
           ffn_conv_b, ffn_w_out, norm_pre_mix, norm_post_mix, norm_pre_ffn, norm_post_ffn):
    batch, seq, d = x.shape
    depth = w_in.shape[0]
    assert w_in.shape[2] == MAIN_COLS + GATE_COLS
    cos, sin_signed = _rope_tables(seq)
    xt = x.reshape(batch * seq, d)
    for l in range(depth):
        w_main = w_in[l, :, :MAIN_COLS].astype(BF16)
        w_gate = jnp.pad(w_in[l, :, MAIN_COLS:], ((0, 0), (0, LANES - GATE_COLS))).astype(BF16)
        proj, gates = _inproj(xt, norm_pre_mix[l][None, :], w_main, w_gate, dn_conv_w[l],
                              _lane_row(dn_a_log[l], N_DELTA_HEADS),
                              _lane_row(dn_dt_bias[l], N_DELTA_HEADS), seq)
        attn = _attention(proj, cos, sin_signed, batch, seq)
        dn = _delta(proj, gates, dn_norm_w[l][None, :], batch, seq)
        xt = _outproj(attn, dn, xt, w_out[l].astype(BF16), norm_post_mix[l][None, :])
        xt = _ffn(xt, norm_pre_ffn[l][None, :], ffn_w_in[l].astype(BF16), ffn_conv_w[l],
                  ffn_conv_b[l][None, :], ffn_w_out[l].astype(BF16), norm_post_ffn[l][None, :], seq)
    return xt.reshape(batch, seq, d)
```

```python
import functools
import math

import numpy as np
import jax
import jax.numpy as jnp
from jax import lax
from jax.experimental import pallas as pl
from jax.experimental.pallas import tpu as pltpu

F32 = jnp.float32
BF16 = jnp.bfloat16

HEAD_DIM = 64
N_ATTN_HEADS = 8
ATTN_WIDTH = N_ATTN_HEADS * HEAD_DIM
DILATED_PATTERNS = ((128, 1), (512, 4), (2048, 16))
ROPE_THETA = 10000.0
MASK_VALUE = -1e30
LOG2_E = math.log2(math.e)

N_DELTA_HEADS = 4
DELTA_DK = 128
DELTA_DV = 128
DELTA_WIDTH = N_DELTA_HEADS * DELTA_DK
DELTA_CONV = 4
DELTA_CHUNK = 64

MAIN_COLS = 3 * ATTN_WIDTH + 4 * DELTA_WIDTH
GATE_COLS = 2 * N_DELTA_HEADS
FFN_CONV = 3
EPS = 1e-6

LANES = 128
SUBLANES = 8
VMEM_LIMIT = 56 * 1024 * 1024

ROW_TILE = 512
FFN_ROW_TILE = 1024
INPROJ_ROW_BLOCKS = 1
FFN_COL_TILE = 256
FFN_ROW_BLOCKS = 4
ATTN_TILE = 512
ATTN_PAIRS_PER_STEP = 2
DELTA_GROUP = 4


def _resident(shape):
    nd = len(shape)
    return pl.BlockSpec(shape, lambda *_: (0,) * nd, pipeline_mode=pl.Buffered(1))


def _layer(shape, layer):
    nd = len(shape)
    return pl.BlockSpec((None,) + tuple(shape), lambda *_: (layer,) + (0,) * nd,
                        pipeline_mode=pl.Buffered(1))


def _params(n_grid, flags=None):
    return pltpu.CompilerParams(dimension_semantics=("arbitrary",) * n_grid,
                                vmem_limit_bytes=VMEM_LIMIT, flags=flags)


def _rms(x, w):
    return x * lax.rsqrt(jnp.mean(x * x, axis=-1, keepdims=True) + EPS) * w


def _sigmoid(x):
    return 1.0 / (1.0 + jnp.exp(-x))


def _dot(a, b):
    return jnp.dot(a, b, preferred_element_type=F32)


def _causal_taps(xcat, x, w_ref, cols, n_taps):
    y = x * w_ref[n_taps - 1:n_taps, cols]
    for s in range(1, n_taps):
        tap = n_taps - 1 - s
        y = y + pltpu.roll(xcat, s, axis=0)[SUBLANES:] * w_ref[tap:tap + 1, cols]
    return y


def _inproj_body(x_ref, nw_ref, w_ref, wg_ref, cw_ref, alog_ref, dtb_ref, proj_ref, gates_ref,
                 carry_scr, *, tiles_per_seq):
    tm = x_ref.shape[0]
    tn = DELTA_WIDTH
    h = _rms(x_ref[...], nw_ref[...]).astype(BF16)
    seq_start = pl.program_id(0) % tiles_per_seq == 0
    first_delta = 3 * ATTN_WIDTH // tn
    carries = [jnp.where(seq_start, 0.0, carry_scr[part]) for part in range(3)]
    new_carries = []
    n_slabs = proj_ref.shape[1] // tn
    plain = [c for c in range(n_slabs) if not first_delta <= c < first_delta + 3]
    order = [c for pair in zip(range(first_delta, first_delta + 3), plain) for c in pair] + plain[3:]
    tb = tm // INPROJ_ROW_BLOCKS
    for c in order:
        cs = slice(c * tn, (c + 1) * tn)
        part = c - first_delta
        prev = carries[part] if 0 <= part < 3 else None
        for r in range(INPROJ_ROW_BLOCKS):
            rs = slice(r * tb, (r + 1) * tb)
            y = _dot(h[rs], w_ref[:, cs])
            if prev is not None:
                ycat = jnp.concatenate([prev, y], axis=0)
                prev = y[tb - SUBLANES:]
                z = _causal_taps(ycat, y, cw_ref, slice(part * tn, (part + 1) * tn), DELTA_CONV)
                z = z * _sigmoid(z)
                if part < 2:
                    scale = DELTA_DK ** -0.5 if part == 0 else 1.0
                    segs = []
                    for hh in range(N_DELTA_HEADS):
                        seg = z[:, hh * DELTA_DK:(hh + 1) * DELTA_DK]
                        inv = lax.rsqrt(jnp.sum(seg * seg, axis=-1, keepdims=True) + EPS)
                        segs.append(seg * (inv * scale))
                    z = jnp.concatenate(segs, axis=1)
                y = z
            proj_ref[rs, cs] = y.astype(BF16)
        if prev is not None:
            new_carries.append(prev)
    g = _dot(h, wg_ref[...])
    sp_in = g + dtb_ref[...]
    softplus = jnp.maximum(sp_in, 0.0) + jnp.log(1.0 + jnp.exp(-jnp.abs(sp_in)))
    lane = lax.broadcasted_iota(jnp.int32, (1, LANES), 1)
    gates_ref[...] = jnp.where(lane < N_DELTA_HEADS, _sigmoid(g), -jnp.exp(alog_ref[...]) * softplus)
    for part in range(3):
        carry_scr[part] = new_carries[part]


def _inproj(x, nw, w_main, w_gate, conv_w, alog, dtb, seq, layer):
    tok, d = x.shape
    assert seq % ROW_TILE == 0
    return pl.pallas_call(
        functools.partial(_inproj_body, tiles_per_seq=seq // ROW_TILE),
        out_shape=(jax.ShapeDtypeStruct((tok, MAIN_COLS), BF16),
                   jax.ShapeDtypeStruct((tok, LANES), F32)),
        grid=(tok // ROW_TILE,),
        in_specs=[pl.BlockSpec((ROW_TILE, d), lambda i: (i, 0)),
                  _layer((1, d), layer),
                  _layer((d, MAIN_COLS), layer),
                  _layer((d, LANES), layer),
                  _layer(conv_w.shape[1:], layer), _layer((1, LANES), layer),
                  _layer((1, LANES), layer)],
        out_specs=(pl.BlockSpec((ROW_TILE, MAIN_COLS), lambda i: (i, 0)),
                   pl.BlockSpec((ROW_TILE, LANES), lambda i: (i, 0))),
        scratch_shapes=[pltpu.VMEM((3, SUBLANES, DELTA_WIDTH), F32)],
        compiler_params=_params(1),
        name="inproj",
    )(x, nw, w_main, w_gate, conv_w, alog, dtb)


def _attn_bias_tiles(seq, t):
    a = np.arange(t)[:, None]
    c = np.arange(t)[None, :]
    tiles = []
    for d in range(seq // t):
        delta = d * t + a - c
        mult = np.zeros((t, t), np.int64)
        for window, dil in DILATED_PATTERNS:
            mult += (delta >= 0) & (delta % dil == 0) & (delta <= window)
        tiles.append(np.where(mult > 0, np.log2(np.maximum(mult, 1)), MASK_VALUE).astype(np.float32))
    n_distinct = 1
    while not all(np.array_equal(tiles[n_distinct - 1], tl) for tl in tiles[n_distinct:]):
        n_distinct += 1
    return np.stack(tiles[:n_distinct]).transpose(0, 2, 1)


def _rope(x, cos, sin_signed):
    lane = lax.broadcasted_iota(jnp.int32, (1, LANES), 1)
    partner = jnp.where((lane % HEAD_DIM) < HEAD_DIM // 2,
                        pltpu.roll(x, LANES - HEAD_DIM // 2, axis=1),
                        pltpu.roll(x, HEAD_DIM // 2, axis=1))
    return x * cos + partner * sin_signed


def _attn_body(q_ref, k_ref, v_ref, cq_ref, sq_ref, ck_ref, sk_ref, bias_ref, o_ref,
               k_scr, vt_scr, qt_scr, m_scr, acc_scr, *, n_bias):
    t = ATTN_TILE
    n_pairs = qt_scr.shape[0]
    i = pl.program_id(2)
    head0 = lax.broadcasted_iota(jnp.int32, (1, LANES), 1) < HEAD_DIM
    top = lax.broadcasted_iota(jnp.int32, (LANES, 1), 0) < HEAD_DIM

    @pl.when(i == 0)
    def _():
        for g in range(n_pairs):
            cs = slice(g * LANES, (g + 1) * LANES)
            for c in range(k_ref.shape[0] // t):
                rs = slice(c * t, (c + 1) * t)
                k = _rope(k_ref[rs, cs].astype(F32), ck_ref[rs, :], sk_ref[rs, :])
                k_scr[2 * g, rs, :] = jnp.where(head0, k, 0.0).astype(BF16)
                k_scr[2 * g + 1, rs, :] = jnp.where(head0, 0.0, k).astype(BF16)
                vt = v_ref[rs, cs].astype(F32).T
                vt_scr[2 * g, :, rs] = jnp.where(top, vt, 1.0).astype(BF16)
                vt_scr[2 * g + 1, :, rs] = jnp.where(top, 1.0, vt).astype(BF16)

    for g in range(n_pairs):
        cs = slice(g * LANES, (g + 1) * LANES)
        q = _rope(q_ref[:, cs].astype(F32), cq_ref[...], sq_ref[...]) * (HEAD_DIM ** -0.5 * LOG2_E)
        qt_scr[g] = q.T.astype(BF16)
    m_scr[...] = jnp.full(m_scr.shape, MASK_VALUE, F32)
    acc_scr[...] = jnp.zeros(acc_scr.shape, F32)

    def step(j, _):
        bias = bias_ref[jnp.minimum(i - j, n_bias - 1)]
        col = pl.multiple_of(j * t, t)
        heads = range(2 * n_pairs)
        ks = [k_scr[hd, pl.ds(col, t), :] for hd in heads]
        vts = [vt_scr[hd, :, pl.ds(col, t)] for hd in heads]
        qts = [qt_scr[g] for g in range(n_pairs)]
        m_prev = [m_scr[hd] for hd in heads]
        acc = [acc_scr[hd] for hd in heads]
        s = [_dot(ks[hd], qts[hd // 2]) + bias for hd in heads]
        m_new = [jnp.maximum(m_prev[hd], jnp.max(s[hd], axis=0, keepdims=True)) for hd in heads]
        p = [jnp.exp2(s[hd] - m_new[hd]).astype(BF16) for hd in heads]
        acc = [acc[hd] * jnp.exp2(m_prev[hd] - m_new[hd]) + _dot(vts[hd], p[hd]) for hd in heads]
        for hd in heads:
            acc_scr[hd] = acc[hd]
            m_scr[hd] = m_new[hd]
        return 0

    lax.fori_loop(0, i + 1, step, 0)
    for g in range(n_pairs):
        a0 = acc_scr[2 * g]
        a1 = acc_scr[2 * g + 1]
        num = jnp.where(top, a0, a1)
        den = jnp.concatenate([a0[HEAD_DIM:], a1[:HEAD_DIM]], axis=0)
        o_ref[:, g * LANES:(g + 1) * LANES] = (num / den).T.astype(BF16)


def _attention(proj, cos, sin_signed, batch, seq):
    t = ATTN_TILE
    nq = seq // t
    bias = _attn_bias_tiles(seq, t)
    gp = ATTN_PAIRS_PER_STEP
    width = gp * LANES
    n_groups = ATTN_WIDTH // width
    return pl.pallas_call(
        functools.partial(_attn_body, n_bias=bias.shape[0]),
        out_shape=jax.ShapeDtypeStruct((batch * seq, ATTN_WIDTH), BF16),
        grid=(batch, n_groups, nq),
        in_specs=[pl.BlockSpec((t, width), lambda b, p, i: (b * nq + i, p)),
                  pl.BlockSpec((seq, width), lambda b, p, i: (b, n_groups + p)),
                  pl.BlockSpec((seq, width), lambda b, p, i: (b, 2 * n_groups + p)),
                  pl.BlockSpec((t, LANES), lambda b, p, i: (i, 0)),
                  pl.BlockSpec((t, LANES), lambda b, p, i: (i, 0)),
                  _resident((seq, LANES)),
                  _resident((seq, LANES)),
                  _resident(bias.shape)],
        out_specs=pl.BlockSpec((t, width), lambda b, p, i: (b * nq + i, p)),
        scratch_shapes=[pltpu.VMEM((2 * gp, seq, LANES), BF16),
                        pltpu.VMEM((2 * gp, LANES, seq), BF16),
                        pltpu.VMEM((gp, LANES, t), BF16),
                        pltpu.VMEM((2 * gp, 1, t), F32),
                        pltpu.VMEM((2 * gp, LANES, t), F32)],
        compiler_params=_params(3),
        name="dilated_attention",
    )(proj, proj, proj, cos, sin_signed, cos, sin_signed, jnp.asarray(bias))


def _dot_nt(a, b):
    return lax.dot_general(a, b, (((1,), (1,)), ((), ())), preferred_element_type=F32)


def _dot_tn(a, b):
    return lax.dot_general(a, b, (((0,), (0,)), ((), ())), preferred_element_type=F32)


def _delta_body(q_ref, k_ref, v_ref, z_ref, gates_ref, nw_ref, o_ref,
                lhs_scr, b_scr, u_scr, qk_scr, gl_scr, state_scr):
    C = DELTA_CHUNK
    H = N_DELTA_HEADS
    G = DELTA_GROUP
    heads = range(H)
    items = [(j, h) for j in range(G) for h in heads]
    n_groups = q_ref.shape[0] // (C * G)
    row = lax.broadcasted_iota(jnp.int32, (C, LANES), 0)
    col = lax.broadcasted_iota(jnp.int32, (C, LANES), 1)
    tril = row >= col
    strict = row > col
    eye = jnp.where(row == col, 1.0, 0.0)
    ones_tril = jnp.where(tril[:, :C], 1.0, 0.0).astype(BF16)
    zeros_cl = jnp.zeros((C, LANES), F32)

    def rows(grp, j):
        return pl.ds(pl.multiple_of((grp * G + j) * C, C), C)

    def lanes(h):
        return slice(h * LANES, (h + 1) * LANES)

    def load_inputs(grp):
        gates = [gates_ref[rows(grp, j), :] for j in range(G)]
        qkv = [{(j, h): ref[rows(grp, j), lanes(h)].astype(F32) for (j, h) in items}
               for ref in (q_ref, k_ref, v_ref)]
        return gates, qkv

    def prepare(gates, qkv):
        q, k, v = qkv
        gc, gc_t = [], []
        for j in range(G):
            g_step = gates[j]
            g_hi = g_step.astype(BF16)
            g_r = g_step - g_hi.astype(F32)
            g_mid = g_r.astype(BF16)
            g_lo = (g_r - g_mid.astype(F32)).astype(BF16)
            gsum = _dot(ones_tril, jnp.concatenate([g_hi, g_mid, g_lo], axis=1))
            gc.append(gsum[:, :LANES] + gsum[:, LANES:2 * LANES] + gsum[:, 2 * LANES:])
            gc_t.append(jnp.concatenate([gc[j], zeros_cl], axis=0).T)
        gi = {(j, h): gc[j][:, H + h:H + h + 1] for (j, h) in items}
        gj = {(j, h): gc_t[j][H + h:H + h + 1, :] for (j, h) in items}
        g_last = {(j, h): gc[j][C - 1:C, H + h:H + h + 1] for (j, h) in items}
        beta = {(j, h): gates[j][:, h:h + 1] for (j, h) in items}
        decay = {it: jnp.exp(jnp.where(tril, gi[it] - gj[it], -jnp.inf)) for it in items}
        kb = {it: k[it] * beta[it] for it in items}
        kq = {it: _dot_nt(jnp.concatenate([kb[it], q[it]], axis=0).astype(BF16),
                          jnp.concatenate([k[it], zeros_cl], axis=0).astype(BF16))
              for it in items}
        a = {it: jnp.where(strict, kq[it][:C] * decay[it], 0.0) for it in items}
        qk = {it: kq[it][C:] * decay[it] for it in items}
        x = {it: eye - a[it] for it in items}
        pb = {it: (-a[it]).astype(BF16) for it in items}
        p = {it: _dot(pb[it][:, :C], pb[it]) for it in items}
        m = 2
        while 2 * m < C:
            pb = {it: p[it].astype(BF16) for it in items}
            both = {it: _dot(pb[it][:, :C], jnp.concatenate([pb[it], x[it].astype(BF16)], axis=1))
                    for it in items}
            p = {it: both[it][:, :LANES] for it in items}
            x = {it: x[it] + both[it][:, LANES:] for it in items}
            m *= 2
        x = {it: x[it] + _dot(p[it].astype(BF16)[:, :C], x[it].astype(BF16)) for it in items}
        eg = {it: jnp.exp(gi[it]) for it in items}
        uw = {it: _dot(x[it].astype(BF16)[:, :C],
                       jnp.concatenate([v[it] * beta[it], kb[it] * eg[it]], axis=1).astype(BF16))
              for it in items}
        kd = {it: (k[it] * jnp.exp(g_last[it] - gi[it])).astype(BF16) for it in items}
        gb = {it: _dot_tn(kd[it], jnp.concatenate([uw[it][:, DELTA_DV:], uw[it][:, :DELTA_DV]],
                                                  axis=1).astype(BF16)) for it in items}
        out = {}
        for it in items:
            u, w = uw[it][:, :DELTA_DV], uw[it][:, DELTA_DV:]
            lhs = jnp.concatenate([gb[it][:, :DELTA_DK], w, q[it] * eg[it]], axis=0).astype(BF16)
            gl = jnp.broadcast_to(jnp.exp(g_last[it]), gl_scr.shape[3:])
            out[it] = (lhs, gb[it][:, DELTA_DK:], u, qk[it].astype(BF16), gl)
        return out

    def store_prepared(slot, prepared):
        for (j, h) in items:
            lhs, bmat, u, qk, gl = prepared[(j, h)]
            lhs_scr[slot, j, h] = lhs
            b_scr[slot, j, h] = bmat
            u_scr[slot, j, h] = u
            qk_scr[slot, j, h] = qk
            gl_scr[slot, j, h] = gl

    def load_prepared(grp, slot):
        prepared = {(j, h): (lhs_scr[slot, j, h], b_scr[slot, j, h], u_scr[slot, j, h],
                             qk_scr[slot, j, h], gl_scr[slot, j, h][0:1, :]) for (j, h) in items}
        z = {(j, h): z_ref[rows(grp, j), lanes(h)].astype(F32) for (j, h) in items}
        return prepared, z, [state_scr[h] for h in heads]

    def consume(prepared, z, state):
        outs = {}
        for j in range(G):
            r = [_dot(prepared[(j, h)][0], state[h].astype(BF16)) for h in heads]
            v_new = [prepared[(j, h)][2] - r[h][DELTA_DK:DELTA_DK + C] for h in heads]
            o = [r[h][DELTA_DK + C:] + _dot(prepared[(j, h)][3][:, :C], v_new[h].astype(BF16))
                 for h in heads]
            state = [prepared[(j, h)][4] * state[h] + prepared[(j, h)][1] - r[h][:DELTA_DK]
                     for h in heads]
            for h in heads:
                zz = z[(j, h)]
                outs[(j, h)] = (_rms(o[h], nw_ref[...]) * zz * _sigmoid(zz)).astype(BF16)
        return state, outs

    state_scr[...] = jnp.zeros(state_scr.shape, F32)
    store_prepared(0, prepare(*load_inputs(0)))

    def step(grp, _):
        slot = grp % 2
        prepared, z, state = load_prepared(grp, slot)
        gates, qkv = load_inputs(jnp.minimum(grp + 1, n_groups - 1))
        state, outs = consume(prepared, z, state)
        nxt = prepare(gates, qkv)
        for h in heads:
            state_scr[h] = state[h]
        for (j, h) in items:
            o_ref[rows(grp, j), lanes(h)] = outs[(j, h)]
        store_prepared(1 - slot, nxt)
        return 0

    lax.fori_loop(0, n_groups, step, 0)


def _delta(proj, gates, norm_w, batch, seq, layer):
    C = DELTA_CHUNK
    H = N_DELTA_HEADS
    G = DELTA_GROUP
    assert seq % (C * G) == 0
    first = 3 * ATTN_WIDTH // DELTA_WIDTH
    blk = lambda j: pl.BlockSpec((seq, DELTA_WIDTH), lambda b: (b, first + j))
    return pl.pallas_call(
        _delta_body,
        out_shape=jax.ShapeDtypeStruct((batch * seq, DELTA_WIDTH), BF16),
        grid=(batch,),
        in_specs=[blk(0), blk(1), blk(2), blk(3),
                  pl.BlockSpec((seq, LANES), lambda b: (b, 0)),
                  _layer((1, DELTA_DV), layer)],
        out_specs=pl.BlockSpec((seq, DELTA_WIDTH), lambda b: (b, 0)),
        scratch_shapes=[pltpu.VMEM((2, G, H, DELTA_DK + 2 * C, DELTA_DK), BF16),
                        pltpu.VMEM((2, G, H, DELTA_DK, DELTA_DV), F32),
                        pltpu.VMEM((2, G, H, C, DELTA_DV), F32),
                        pltpu.VMEM((2, G, H, C, LANES), BF16),
                        pltpu.VMEM((2, G, H, SUBLANES, DELTA_DV), F32),
                        pltpu.VMEM((H, DELTA_DK, DELTA_DV), F32)],
        compiler_params=_params(1),
        name="gated_delta",
    )(proj, proj, proj, proj, gates, norm_w)


def _outproj_body(a_ref, d_ref, x_ref, w_ref, nw_ref, o_ref):
    wa = a_ref.shape[1]
    mix = _dot(a_ref[...], w_ref[:wa, :]) + _dot(d_ref[...], w_ref[wa:, :])
    o_ref[...] = x_ref[...] + _rms(mix, nw_ref[...])


def _outproj(attn, dn, x, w, nw, layer):
    tok, d = x.shape
    return pl.pallas_call(
        _outproj_body,
        out_shape=jax.ShapeDtypeStruct((tok, d), F32),
        grid=(tok // ROW_TILE,),
        in_specs=[pl.BlockSpec((ROW_TILE, attn.shape[1]), lambda i: (i, 0)),
                  pl.BlockSpec((ROW_TILE, dn.shape[1]), lambda i: (i, 0)),
                  pl.BlockSpec((ROW_TILE, d), lambda i: (i, 0)),
                  _layer(w.shape[1:], layer), _layer((1, d), layer)],
        out_specs=pl.BlockSpec((ROW_TILE, d), lambda i: (i, 0)),
        compiler_params=_params(1),
        name="outproj",
    )(attn, dn, x, w, nw)


def _gelu_tanh(x):
    a = -2.0 * math.sqrt(2.0 / math.pi) * math.log2(math.e)
    b = 0.044715 * a
    return x / (1.0 + jnp.exp2(x * (a + b * (x * x))))


def _ffn_body(x_ref, nw_pre_ref, w_in_ref, cw_ref, cb_ref, w_out_ref, nw_post_ref, o_ref,
              h_scr, acc_scr, carry_scr, u_scr, *, tiles_per_seq):
    tm = x_ref.shape[0]
    d_ff = w_out_ref.shape[0]
    tf = FFN_COL_TILE
    n_chunks = d_ff // tf
    x = x_ref[...]
    h_scr[...] = _rms(x, nw_pre_ref[...]).astype(BF16)
    seq_start = pl.program_id(0) % tiles_per_seq == 0

    def cols(c, branch):
        return pl.ds(pl.multiple_of(branch * d_ff + c * tf, tf), tf)

    tb = tm // FFN_ROW_BLOCKS

    def up_proj(c, r):
        h = h_scr[r * tb:(r + 1) * tb, :]
        return [_dot(h, w_in_ref[:, cols(c, br)]) for br in range(2)]

    def prev_rows(c):
        return [jnp.where(seq_start, 0.0, carry_scr[:, cols(c, br)]) for br in range(2)]

    def store_u(c_prev_rows, blocks):
        for br in range(2):
            u_scr[br, :SUBLANES, :] = c_prev_rows[br]
            for r in range(FFN_ROW_BLOCKS):
                u_scr[br, SUBLANES + r * tb:SUBLANES + (r + 1) * tb, :] = blocks[r][br]

    def activation(c, r):
        y = []
        for br in range(2):
            cs = cols(c, br)
            ucat = u_scr[br, r * tb:SUBLANES + (r + 1) * tb, :]
            y.append(_causal_taps(ucat, ucat[SUBLANES:], cw_ref, cs, FFN_CONV) + cb_ref[:, cs])
        return (_gelu_tanh(y[0]) * y[1]).astype(BF16)

    def run_chunk(c, first, has_next):
        w_down = w_out_ref[pl.ds(pl.multiple_of(c * tf, tf), tf), :]
        last_rows = [u_scr[br, tm:, :] for br in range(2)]
        nxt_prev = prev_rows(c + 1) if has_next else None
        acc, nxt = [], []
        for r in range(FFN_ROW_BLOCKS):
            if has_next:
                nxt.append(up_proj(c + 1, r))
            down = _dot(activation(c, r), w_down)
            acc.append(down if first else acc_scr[r * tb:(r + 1) * tb, :] + down)
        for br in range(2):
            carry_scr[:, cols(c, br)] = last_rows[br]
        if has_next:
            store_u(nxt_prev, nxt)
        return acc

    def store_acc(acc):
        for r in range(FFN_ROW_BLOCKS):
            acc_scr[r * tb:(r + 1) * tb, :] = acc[r]

    store_u(prev_rows(0), [up_proj(0, r) for r in range(FFN_ROW_BLOCKS)])
    store_acc(run_chunk(0, True, True))

    def step(c, _):
        store_acc(run_chunk(c, False, True))
        return 0

    lax.fori_loop(1, n_chunks - 1, step, 0)
    f = jnp.concatenate(run_chunk(n_chunks - 1, False, False), axis=0)
    o_ref[...] = x + _rms(f, nw_post_ref[...])


def _ffn(x, nw_pre, w_in, conv_w, conv_b, w_out, nw_post, seq, layer):
    tok, d = x.shape
    d_ff = w_out.shape[1]
    tm = FFN_ROW_TILE
    assert d_ff % FFN_COL_TILE == 0 and d_ff // FFN_COL_TILE >= 3 and seq % tm == 0
    return pl.pallas_call(
        functools.partial(_ffn_body, tiles_per_seq=seq // tm),
        out_shape=jax.ShapeDtypeStruct((tok, d), F32),
        grid=(tok // tm,),
        in_specs=[pl.BlockSpec((tm, d), lambda i: (i, 0)),
                  _layer((1, d), layer), _layer(w_in.shape[1:], layer),
                  _layer(conv_w.shape[1:], layer), _layer((1, 2 * d_ff), layer),
                  _layer(w_out.shape[1:], layer), _layer((1, d), layer)],
        out_specs=pl.BlockSpec((tm, d), lambda i: (i, 0)),
        scratch_shapes=[pltpu.VMEM((tm, d), BF16),
                        pltpu.VMEM((tm, d), F32),
                        pltpu.VMEM((SUBLANES, 2 * d_ff), F32),
                        pltpu.VMEM((2, SUBLANES + tm, FFN_COL_TILE), F32)],
        compiler_params=_params(1),
        name="conv_glu_ffn",
    )(x, nw_pre, w_in, conv_w, conv_b, w_out, nw_post)


def _rope_tables(seq):
    half = HEAD_DIM // 2
    inv = 1.0 / (ROPE_THETA ** (jnp.arange(0, HEAD_DIM, 2, dtype=F32) / HEAD_DIM))
    ang = jnp.arange(seq, dtype=F32)[:, None] * inv[None, :]
    cos, sin = jnp.cos(ang), jnp.sin(ang)
    reps = LANES // half
    sign = jnp.tile(jnp.concatenate([-jnp.ones((half,), F32), jnp.ones((half,), F32)]), reps // 2)
    return jnp.tile(cos, (1, reps)), jnp.tile(sin, (1, reps)) * sign[None, :]


def _lane_rows(vals, offset):
    depth, n = vals.shape
    return jnp.zeros((depth, 1, LANES), F32).at[:, 0, offset:offset + n].set(vals.astype(F32))


def kernel(x, w_in, dn_conv_w, dn_a_log, dn_dt_bias, dn_norm_w, w_out, ffn_w_in, ffn_conv_w,
           ffn_conv_b, ffn_w_out, norm_pre_mix, norm_post_mix, norm_pre_ffn, norm_post_ffn):
    batch, seq, d = x.shape
    depth = w_in.shape[0]
    assert w_in.shape[2] == MAIN_COLS + GATE_COLS
    cos, sin_signed = _rope_tables(seq)
    xt = x.reshape(batch * seq, d)
    row = lambda p: p[:, None, :]
    w_in_b = w_in.astype(BF16)
    w_gate_b = jnp.pad(w_in[:, :, MAIN_COLS:], ((0, 0), (0, 0), (0, LANES - GATE_COLS))).astype(BF16)
    w_out_b = w_out.astype(BF16)
    ffn_w_in_b = ffn_w_in.astype(BF16)
    ffn_w_out_b = ffn_w_out.astype(BF16)
    alog = _lane_rows(dn_a_log, N_DELTA_HEADS)
    dtb = _lane_rows(dn_dt_bias, N_DELTA_HEADS)
    for l in range(depth):
        proj, gates = _inproj(xt, row(norm_pre_mix), w_in_b, w_gate_b, dn_conv_w, alog, dtb, seq, l)
        attn = _attention(proj, cos, sin_signed, batch, seq)
        dn = _delta(proj, gates, row(dn_norm_w), batch, seq, l)
        xt = _outproj(attn, dn, xt, w_out_b, row(norm_post_mix), l)
        xt = _ffn(xt, row(norm_pre_ffn), ffn_w_in_b, ffn_conv_w, row(ffn_conv_b), ffn_w_out_b,
                  row(norm_post_ffn), seq, l)
    return xt.reshape(batch, seq, d)
```

```python
import functools
import math

import numpy as np
import jax
import jax.numpy as jnp
from jax import lax
from jax.experimental import pallas as pl
from jax.experimental.pallas import tpu as pltpu

F32 = jnp.float32
BF16 = jnp.bfloat16

HEAD_DIM = 64
N_ATTN_HEADS = 8
ATTN_WIDTH = N_ATTN_HEADS * HEAD_DIM
DILATED_PATTERNS = ((128, 1), (512, 4), (2048, 16))
ROPE_THETA = 10000.0
MASK_VALUE = -1e30
LOG2_E = math.log2(math.e)

N_DELTA_HEADS = 4
DELTA_DK = 128
DELTA_DV = 128
DELTA_WIDTH = N_DELTA_HEADS * DELTA_DK
DELTA_CONV = 4
DELTA_CHUNK = 64

MAIN_COLS = 3 * ATTN_WIDTH + 4 * DELTA_WIDTH
GATE_COLS = 2 * N_DELTA_HEADS
FFN_CONV = 3
EPS = 1e-6

LANES = 128
SUBLANES = 8
VMEM_LIMIT = 56 * 1024 * 1024

ROW_TILE = 512
FFN_ROW_TILE = 1024
INPROJ_ROW_BLOCKS = 1
FFN_COL_TILE = 256
FFN_ROW_BLOCKS = 4
ATTN_TILE = 512
ATTN_DEN_ROWS = 16
ATTN_PAIRS_PER_STEP = 2
DELTA_GROUP = 4


def _resident(shape):
    nd = len(shape)
    return pl.BlockSpec(shape, lambda *_: (0,) * nd, pipeline_mode=pl.Buffered(1))


def _layer(shape, layer):
    nd = len(shape)
    return pl.BlockSpec((None,) + tuple(shape), lambda *_: (layer,) + (0,) * nd,
                        pipeline_mode=pl.Buffered(1))


def _params(n_grid, flags=None):
    return pltpu.CompilerParams(dimension_semantics=("arbitrary",) * n_grid,
                                vmem_limit_bytes=VMEM_LIMIT, flags=flags)


def _rms(x, w):
    return x * lax.rsqrt(jnp.mean(x * x, axis=-1, keepdims=True) + EPS) * w


def _sigmoid(x):
    return 1.0 / (1.0 + jnp.exp(-x))


def _dot(a, b):
    return jnp.dot(a, b, preferred_element_type=F32)


def _causal_taps(xcat, x, w_ref, cols, n_taps):
    y = x * w_ref[n_taps - 1:n_taps, cols]
    for s in range(1, n_taps):
        tap = n_taps - 1 - s
        y = y + pltpu.roll(xcat, s, axis=0)[SUBLANES:] * w_ref[tap:tap + 1, cols]
    return y


def _inproj_body(x_ref, nw_ref, w_ref, wg_ref, cw_ref, alog_ref, dtb_ref, proj_ref, gates_ref,
                 carry_scr, *, tiles_per_seq):
    tm = x_ref.shape[0]
    tn = DELTA_WIDTH
    h = _rms(x_ref[...], nw_ref[...]).astype(BF16)
    seq_start = pl.program_id(0) % tiles_per_seq == 0
    first_delta = 3 * ATTN_WIDTH // tn
    carries = [jnp.where(seq_start, 0.0, carry_scr[part]) for part in range(3)]
    new_carries = []
    n_slabs = proj_ref.shape[1] // tn
    plain = [c for c in range(n_slabs) if not first_delta <= c < first_delta + 3]
    order = [c for pair in zip(range(first_delta, first_delta + 3), plain) for c in pair] + plain[3:]
    tb = tm // INPROJ_ROW_BLOCKS
    for c in order:
        cs = slice(c * tn, (c + 1) * tn)
        part = c - first_delta
        prev = carries[part] if 0 <= part < 3 else None
        for r in range(INPROJ_ROW_BLOCKS):
            rs = slice(r * tb, (r + 1) * tb)
            y = _dot(h[rs], w_ref[:, cs])
            if prev is not None:
                ycat = jnp.concatenate([prev, y], axis=0)
                prev = y[tb - SUBLANES:]
                z = _causal_taps(ycat, y, cw_ref, slice(part * tn, (part + 1) * tn), DELTA_CONV)
                z = z * _sigmoid(z)
                if part < 2:
                    scale = DELTA_DK ** -0.5 if part == 0 else 1.0
                    segs = []
                    for hh in range(N_DELTA_HEADS):
                        seg = z[:, hh * DELTA_DK:(hh + 1) * DELTA_DK]
                        inv = lax.rsqrt(jnp.sum(seg * seg, axis=-1, keepdims=True) + EPS)
                        segs.append(seg * (inv * scale))
                    z = jnp.concatenate(segs, axis=1)
                y = z
            proj_ref[rs, cs] = y.astype(BF16)
        if prev is not None:
            new_carries.append(prev)
    g = _dot(h, wg_ref[...])
    sp_in = g + dtb_ref[...]
    softplus = jnp.maximum(sp_in, 0.0) + jnp.log(1.0 + jnp.exp(-jnp.abs(sp_in)))
    lane = lax.broadcasted_iota(jnp.int32, (1, LANES), 1)
    gates_ref[...] = jnp.where(lane < N_DELTA_HEADS, _sigmoid(g), -jnp.exp(alog_ref[...]) * softplus)
    for part in range(3):
        carry_scr[part] = new_carries[part]


def _inproj(x, nw, w, conv_w, alog, dtb, seq, layer):
    tok, d = x.shape
    assert seq % ROW_TILE == 0 and w.shape[2] == MAIN_COLS + LANES
    gate_block = pl.BlockSpec((None, d, LANES), lambda *_: (layer, 0, MAIN_COLS // LANES),
                              pipeline_mode=pl.Buffered(1))
    return pl.pallas_call(
        functools.partial(_inproj_body, tiles_per_seq=seq // ROW_TILE),
        out_shape=(jax.ShapeDtypeStruct((tok, MAIN_COLS), BF16),
                   jax.ShapeDtypeStruct((tok, LANES), F32)),
        grid=(tok // ROW_TILE,),
        in_specs=[pl.BlockSpec((ROW_TILE, d), lambda i: (i, 0)),
                  _layer((1, d), layer),
                  _layer((d, MAIN_COLS), layer),
                  gate_block,
                  _layer(conv_w.shape[1:], layer), _layer((1, LANES), layer),
                  _layer((1, LANES), layer)],
        out_specs=(pl.BlockSpec((ROW_TILE, MAIN_COLS), lambda i: (i, 0)),
                   pl.BlockSpec((ROW_TILE, LANES), lambda i: (i, 0))),
        scratch_shapes=[pltpu.VMEM((3, SUBLANES, DELTA_WIDTH), F32)],
        compiler_params=_params(1),
        name="inproj",
    )(x, nw, w, w, conv_w, alog, dtb)


def _attn_bias_tiles(seq, t):
    a = np.arange(t)[:, None]
    c = np.arange(t)[None, :]
    tiles = []
    for d in range(seq // t):
        delta = d * t + a - c
        mult = np.zeros((t, t), np.int64)
        for window, dil in DILATED_PATTERNS:
            mult += (delta >= 0) & (delta % dil == 0) & (delta <= window)
        tiles.append(np.where(mult > 0, np.log2(np.maximum(mult, 1)), MASK_VALUE).astype(np.float32))
    n_distinct = 1
    while not all(np.array_equal(tiles[n_distinct - 1], tl) for tl in tiles[n_distinct:]):
        n_distinct += 1
    return np.stack(tiles[:n_distinct]).transpose(0, 2, 1)


def _rope(x, cos, sin_signed):
    lane = lax.broadcasted_iota(jnp.int32, (1, LANES), 1)
    partner = jnp.where((lane % HEAD_DIM) < HEAD_DIM // 2,
                        pltpu.roll(x, LANES - HEAD_DIM // 2, axis=1),
                        pltpu.roll(x, HEAD_DIM // 2, axis=1))
    return x * cos + partner * sin_signed


def _attn_body(q_ref, k_ref, v_ref, cq_ref, sq_ref, ck_ref, sk_ref, bias_ref, o_ref,
               k_scr, vt_scr, qt_scr, m_scr, acc_scr, *, n_bias):
    t = ATTN_TILE
    n_pairs = qt_scr.shape[0]
    i = pl.program_id(2)
    head0 = lax.broadcasted_iota(jnp.int32, (1, LANES), 1) < HEAD_DIM

    @pl.when(i == 0)
    def _():
        for g in range(n_pairs):
            cs = slice(g * LANES, (g + 1) * LANES)
            for c in range(k_ref.shape[0] // t):
                rs = slice(c * t, (c + 1) * t)
                k = _rope(k_ref[rs, cs].astype(F32), ck_ref[rs, :], sk_ref[rs, :])
                k_scr[2 * g, rs, :] = jnp.where(head0, k, 0.0).astype(BF16)
                k_scr[2 * g + 1, rs, :] = jnp.where(head0, 0.0, k).astype(BF16)
                vt = v_ref[rs, cs].astype(F32).T.astype(BF16)
                ones = jnp.ones((ATTN_DEN_ROWS, t), BF16)
                vt_scr[2 * g, :HEAD_DIM, rs] = vt[:HEAD_DIM]
                vt_scr[2 * g + 1, :HEAD_DIM, rs] = vt[HEAD_DIM:]
                vt_scr[2 * g, HEAD_DIM:, rs] = ones
                vt_scr[2 * g + 1, HEAD_DIM:, rs] = ones

    for g in range(n_pairs):
        cs = slice(g * LANES, (g + 1) * LANES)
        q = _rope(q_ref[:, cs].astype(F32), cq_ref[...], sq_ref[...]) * (HEAD_DIM ** -0.5 * LOG2_E)
        qt_scr[g] = q.T.astype(BF16)
    m_scr[...] = jnp.full(m_scr.shape, MASK_VALUE, F32)
    acc_scr[...] = jnp.zeros(acc_scr.shape, F32)

    def step(j, _):
        bias = bias_ref[jnp.minimum(i - j, n_bias - 1)]
        col = pl.multiple_of(j * t, t)
        heads = range(2 * n_pairs)
        ks = [k_scr[hd, pl.ds(col, t), :] for hd in heads]
        vts = [vt_scr[hd, :, pl.ds(col, t)] for hd in heads]
        qts = [qt_scr[g] for g in range(n_pairs)]
        m_prev = [m_scr[hd] for hd in heads]
        acc = [acc_scr[hd] for hd in heads]
        s = [_dot(ks[hd], qts[hd // 2]) + bias for hd in heads]
        m_new = [jnp.maximum(m_prev[hd], jnp.max(s[hd], axis=0, keepdims=True)) for hd in heads]
        p = [jnp.exp2(s[hd] - m_new[hd]).astype(BF16) for hd in heads]
        acc = [acc[hd] * jnp.exp2(m_prev[hd] - m_new[hd]) + _dot(vts[hd], p[hd]) for hd in heads]
        for hd in heads:
            acc_scr[hd] = acc[hd]
            m_scr[hd] = m_new[hd]
        return 0

    lax.fori_loop(0, i + 1, step, 0)
    for g in range(n_pairs):
        halves = []
        for hd in (2 * g, 2 * g + 1):
            a = acc_scr[hd]
            halves.append(a[:HEAD_DIM] / a[HEAD_DIM:HEAD_DIM + 1])
        o_ref[:, g * LANES:(g + 1) * LANES] = jnp.concatenate(halves, axis=0).T.astype(BF16)


def _attention(proj, cos, sin_signed, batch, seq):
    t = ATTN_TILE
    nq = seq // t
    bias = _attn_bias_tiles(seq, t)
    gp = ATTN_PAIRS_PER_STEP
    width = gp * LANES
    n_groups = ATTN_WIDTH // width
    return pl.pallas_call(
        functools.partial(_attn_body, n_bias=bias.shape[0]),
        out_shape=jax.ShapeDtypeStruct((batch * seq, ATTN_WIDTH), BF16),
        grid=(batch, n_groups, nq),
        in_specs=[pl.BlockSpec((t, width), lambda b, p, i: (b * nq + i, p)),
                  pl.BlockSpec((seq, width), lambda b, p, i: (b, n_groups + p)),
                  pl.BlockSpec((seq, width), lambda b, p, i: (b, 2 * n_groups + p)),
                  pl.BlockSpec((t, LANES), lambda b, p, i: (i, 0)),
                  pl.BlockSpec((t, LANES), lambda b, p, i: (i, 0)),
                  _resident((seq, LANES)),
                  _resident((seq, LANES)),
                  _resident(bias.shape)],
        out_specs=pl.BlockSpec((t, width), lambda b, p, i: (b * nq + i, p)),
        scratch_shapes=[pltpu.VMEM((2 * gp, seq, LANES), BF16),
                        pltpu.VMEM((2 * gp, HEAD_DIM + ATTN_DEN_ROWS, seq), BF16),
                        pltpu.VMEM((gp, LANES, t), BF16),
                        pltpu.VMEM((2 * gp, 1, t), F32),
                        pltpu.VMEM((2 * gp, HEAD_DIM + ATTN_DEN_ROWS, t), F32)],
        compiler_params=_params(3),
        name="dilated_attention",
    )(proj, proj, proj, cos, sin_signed, cos, sin_signed, jnp.asarray(bias))


def _dot_nt(a, b):
    return lax.dot_general(a, b, (((1,), (1,)), ((), ())), preferred_element_type=F32)


def _dot_tn(a, b):
    return lax.dot_general(a, b, (((0,), (0,)), ((), ())), preferred_element_type=F32)


def _delta_body(q_ref, k_ref, v_ref, z_ref, gates_ref, nw_ref, o_ref,
                lhs_scr, b_scr, u_scr, qk_scr, gl_scr, state_scr):
    C = DELTA_CHUNK
    H = N_DELTA_HEADS
    G = DELTA_GROUP
    heads = range(H)
    items = [(j, h) for j in range(G) for h in heads]
    n_groups = q_ref.shape[0] // (C * G)
    row = lax.broadcasted_iota(jnp.int32, (C, LANES), 0)
    col = lax.broadcasted_iota(jnp.int32, (C, LANES), 1)
    tril = row >= col
    strict = row > col
    eye = jnp.where(row == col, 1.0, 0.0)
    ones_tril = jnp.where(tril[:, :C], 1.0, 0.0).astype(BF16)
    zeros_cl = jnp.zeros((C, LANES), F32)

    def rows(grp, j):
        return pl.ds(pl.multiple_of((grp * G + j) * C, C), C)

    def lanes(h):
        return slice(h * LANES, (h + 1) * LANES)

    def load_inputs(grp):
        gates = [gates_ref[rows(grp, j), :] for j in range(G)]
        qkv = [{(j, h): ref[rows(grp, j), lanes(h)].astype(F32) for (j, h) in items}
               for ref in (q_ref, k_ref, v_ref)]
        return gates, qkv

    def prepare(gates, qkv):
        q, k, v = qkv
        gc, gc_t = [], []
        for j in range(G):
            g_step = gates[j]
            g_hi = g_step.astype(BF16)
            g_r = g_step - g_hi.astype(F32)
            g_mid = g_r.astype(BF16)
            g_lo = (g_r - g_mid.astype(F32)).astype(BF16)
            gsum = _dot(ones_tril, jnp.concatenate([g_hi, g_mid, g_lo], axis=1))
            gc.append(gsum[:, :LANES] + gsum[:, LANES:2 * LANES] + gsum[:, 2 * LANES:])
            gc_t.append(jnp.concatenate([gc[j], zeros_cl], axis=0).T)
        gi = {(j, h): gc[j][:, H + h:H + h + 1] for (j, h) in items}
        gj = {(j, h): gc_t[j][H + h:H + h + 1, :] for (j, h) in items}
        g_last = {(j, h): gc[j][C - 1:C, H + h:H + h + 1] for (j, h) in items}
        beta = {(j, h): gates[j][:, h:h + 1] for (j, h) in items}
        decay = {it: jnp.exp(jnp.where(tril, gi[it] - gj[it], -jnp.inf)) for it in items}
        kb = {it: k[it] * beta[it] for it in items}
        kq = {it: _dot_nt(jnp.concatenate([kb[it], q[it]], axis=0).astype(BF16),
                          jnp.concatenate([k[it], zeros_cl], axis=0).astype(BF16))
              for it in items}
        a = {it: jnp.where(strict, kq[it][:C] * decay[it], 0.0) for it in items}
        qk = {it: kq[it][C:] * decay[it] for it in items}
        x = {it: eye - a[it] for it in items}
        pb = {it: (-a[it]).astype(BF16) for it in items}
        p = {it: _dot(pb[it][:, :C], pb[it]) for it in items}
        m = 2
        while 2 * m < C:
            pb = {it: p[it].astype(BF16) for it in items}
            both = {it: _dot(pb[it][:, :C], jnp.concatenate([pb[it], x[it].astype(BF16)], axis=1))
                    for it in items}
            p = {it: both[it][:, :LANES] for it in items}
            x = {it: x[it] + both[it][:, LANES:] for it in items}
            m *= 2
        x = {it: x[it] + _dot(p[it].astype(BF16)[:, :C], x[it].astype(BF16)) for it in items}
        eg = {it: jnp.exp(gi[it]) for it in items}
        uw = {it: _dot(x[it].astype(BF16)[:, :C],
                       jnp.concatenate([v[it] * beta[it], kb[it] * eg[it]], axis=1).astype(BF16))
              for it in items}
        kd = {it: (k[it] * jnp.exp(g_last[it] - gi[it])).astype(BF16) for it in items}
        gb = {it: _dot_tn(kd[it], jnp.concatenate([uw[it][:, DELTA_DV:], uw[it][:, :DELTA_DV]],
                                                  axis=1).astype(BF16)) for it in items}
        out = {}
        for it in items:
            u, w = uw[it][:, :DELTA_DV], uw[it][:, DELTA_DV:]
            lhs = jnp.concatenate([gb[it][:, :DELTA_DK], w, q[it] * eg[it]], axis=0).astype(BF16)
            gl = jnp.broadcast_to(jnp.exp(g_last[it]), gl_scr.shape[3:])
            out[it] = (lhs, gb[it][:, DELTA_DK:], u, qk[it].astype(BF16), gl)
        return out

    def store_prepared(slot, prepared):
        for (j, h) in items:
            lhs, bmat, u, qk, gl = prepared[(j, h)]
            lhs_scr[slot, j, h] = lhs
            b_scr[slot, j, h] = bmat
            u_scr[slot, j, h] = u
            qk_scr[slot, j, h] = qk
            gl_scr[slot, j, h] = gl

    def load_prepared(grp, slot):
        prepared = {(j, h): (lhs_scr[slot, j, h], b_scr[slot, j, h], u_scr[slot, j, h],
                             qk_scr[slot, j, h], gl_scr[slot, j, h][0:1, :]) for (j, h) in items}
        z = {(j, h): z_ref[rows(grp, j), lanes(h)].astype(F32) for (j, h) in items}
        return prepared, z, [state_scr[h] for h in heads]

    def consume(prepared, z, state):
        outs = {}
        for j in range(G):
            r = [_dot(prepared[(j, h)][0], state[h].astype(BF16)) for h in heads]
            v_new = [prepared[(j, h)][2] - r[h][DELTA_DK:DELTA_DK + C] for h in heads]
            o = [r[h][DELTA_DK + C:] + _dot(prepared[(j, h)][3][:, :C], v_new[h].astype(BF16))
                 for h in heads]
            state = [prepared[(j, h)][4] * state[h] + prepared[(j, h)][1] - r[h][:DELTA_DK]
                     for h in heads]
            for h in heads:
                zz = z[(j, h)]
                outs[(j, h)] = (_rms(o[h], nw_ref[...]) * zz * _sigmoid(zz)).astype(BF16)
        return state, outs

    state_scr[...] = jnp.zeros(state_scr.shape, F32)
    store_prepared(0, prepare(*load_inputs(0)))

    def step(grp, _):
        slot = grp % 2
        prepared, z, state = load_prepared(grp, slot)
        gates, qkv = load_inputs(jnp.minimum(grp + 1, n_groups - 1))
        state, outs = consume(prepared, z, state)
        nxt = prepare(gates, qkv)
        for h in heads:
            state_scr[h] = state[h]
        for (j, h) in items:
            o_ref[rows(grp, j), lanes(h)] = outs[(j, h)]
        store_prepared(1 - slot, nxt)
        return 0

    lax.fori_loop(0, n_groups, step, 0)


def _delta(proj, gates, norm_w, batch, seq, layer):
    C = DELTA_CHUNK
    H = N_DELTA_HEADS
    G = DELTA_GROUP
    assert seq % (C * G) == 0
    first = 3 * ATTN_WIDTH // DELTA_WIDTH
    blk = lambda j: pl.BlockSpec((seq, DELTA_WIDTH), lambda b: (b, first + j))
    return pl.pallas_call(
        _delta_body,
        out_shape=jax.ShapeDtypeStruct((batch * seq, DELTA_WIDTH), BF16),
        grid=(batch,),
        in_specs=[blk(0), blk(1), blk(2), blk(3),
                  pl.BlockSpec((seq, LANES), lambda b: (b, 0)),
                  _layer((1, DELTA_DV), layer)],
        out_specs=pl.BlockSpec((seq, DELTA_WIDTH), lambda b: (b, 0)),
        scratch_shapes=[pltpu.VMEM((2, G, H, DELTA_DK + 2 * C, DELTA_DK), BF16),
                        pltpu.VMEM((2, G, H, DELTA_DK, DELTA_DV), F32),
                        pltpu.VMEM((2, G, H, C, DELTA_DV), F32),
                        pltpu.VMEM((2, G, H, C, LANES), BF16),
                        pltpu.VMEM((2, G, H, SUBLANES, DELTA_DV), F32),
                        pltpu.VMEM((H, DELTA_DK, DELTA_DV), F32)],
        compiler_params=_params(1),
        name="gated_delta",
    )(proj, proj, proj, proj, gates, norm_w)


def _gelu_tanh(x):
    a = -2.0 * math.sqrt(2.0 / math.pi) * math.log2(math.e)
    b = 0.044715 * a
    return x / (1.0 + jnp.exp2(x * (a + b * (x * x))))


def _ffn_body(attn_ref, dn_ref, x_ref, w_mix_ref, nw_mix_ref, nw_pre_ref, w_in_ref, cw_ref, cb_ref,
              w_out_ref, nw_post_ref, o_ref, h_scr, acc_scr, carry_scr, u_scr, *, tiles_per_seq):
    tm = x_ref.shape[0]
    d_ff = w_out_ref.shape[0]
    tf = FFN_COL_TILE
    n_chunks = d_ff // tf
    wa = attn_ref.shape[1]
    mix = _dot(attn_ref[...], w_mix_ref[:wa, :]) + _dot(dn_ref[...], w_mix_ref[wa:, :])
    x = x_ref[...] + _rms(mix, nw_mix_ref[...])
    o_ref[...] = x
    h_scr[...] = _rms(x, nw_pre_ref[...]).astype(BF16)
    seq_start = pl.program_id(0) % tiles_per_seq == 0

    def cols(c, branch):
        return pl.ds(pl.multiple_of(branch * d_ff + c * tf, tf), tf)

    tb = tm // FFN_ROW_BLOCKS

    def up_proj(c, r):
        h = h_scr[r * tb:(r + 1) * tb, :]
        return [_dot(h, w_in_ref[:, cols(c, br)]) for br in range(2)]

    def prev_rows(c):
        return [jnp.where(seq_start, 0.0, carry_scr[:, cols(c, br)]) for br in range(2)]

    def store_u(c_prev_rows, blocks):
        for br in range(2):
            u_scr[br, :SUBLANES, :] = c_prev_rows[br]
            for r in range(FFN_ROW_BLOCKS):
                u_scr[br, SUBLANES + r * tb:SUBLANES + (r + 1) * tb, :] = blocks[r][br]

    def activation(c, r):
        y = []
        for br in range(2):
            cs = cols(c, br)
            ucat = u_scr[br, r * tb:SUBLANES + (r + 1) * tb, :]
            y.append(_causal_taps(ucat, ucat[SUBLANES:], cw_ref, cs, FFN_CONV) + cb_ref[:, cs])
        return (_gelu_tanh(y[0]) * y[1]).astype(BF16)

    def run_chunk(c, first, has_next):
        w_down = w_out_ref[pl.ds(pl.multiple_of(c * tf, tf), tf), :]
        last_rows = [u_scr[br, tm:, :] for br in range(2)]
        nxt_prev = prev_rows(c + 1) if has_next else None
        acc, nxt = [], []
        for r in range(FFN_ROW_BLOCKS):
            if has_next:
                nxt.append(up_proj(c + 1, r))
            down = _dot(activation(c, r), w_down)
            acc.append(down if first else acc_scr[r * tb:(r + 1) * tb, :] + down)
        for br in range(2):
            carry_scr[:, cols(c, br)] = last_rows[br]
        if has_next:
            store_u(nxt_prev, nxt)
        return acc

    def store_acc(acc):
        for r in range(FFN_ROW_BLOCKS):
            acc_scr[r * tb:(r + 1) * tb, :] = acc[r]

    store_u(prev_rows(0), [up_proj(0, r) for r in range(FFN_ROW_BLOCKS)])
    store_acc(run_chunk(0, True, True))

    def step(c, _):
        store_acc(run_chunk(c, False, True))
        return 0

    lax.fori_loop(1, n_chunks - 1, step, 0)
    f = jnp.concatenate(run_chunk(n_chunks - 1, False, False), axis=0)
    o_ref[...] = o_ref[...] + _rms(f, nw_post_ref[...])


def _ffn(attn, dn, x, w_mix, nw_mix, nw_pre, w_in, conv_w, conv_b, w_out, nw_post, seq, layer):
    tok, d = x.shape
    d_ff = w_out.shape[1]
    tm = FFN_ROW_TILE
    assert d_ff % FFN_COL_TILE == 0 and d_ff // FFN_COL_TILE >= 3 and seq % tm == 0
    return pl.pallas_call(
        functools.partial(_ffn_body, tiles_per_seq=seq // tm),
        out_shape=jax.ShapeDtypeStruct((tok, d), F32),
        grid=(tok // tm,),
        in_specs=[pl.BlockSpec((tm, attn.shape[1]), lambda i: (i, 0)),
                  pl.BlockSpec((tm, dn.shape[1]), lambda i: (i, 0)),
                  pl.BlockSpec((tm, d), lambda i: (i, 0)),
                  _layer(w_mix.shape[1:], layer), _layer((1, d), layer),
                  _layer((1, d), layer), _layer(w_in.shape[1:], layer),
                  _layer(conv_w.shape[1:], layer), _layer((1, 2 * d_ff), layer),
                  _layer(w_out.shape[1:], layer), _layer((1, d), layer)],
        out_specs=pl.BlockSpec((tm, d), lambda i: (i, 0)),
        scratch_shapes=[pltpu.VMEM((tm, d), BF16),
                        pltpu.VMEM((tm, d), F32),
                        pltpu.VMEM((SUBLANES, 2 * d_ff), F32),
                        pltpu.VMEM((2, SUBLANES + tm, FFN_COL_TILE), F32)],
        compiler_params=_params(1),
        name="conv_glu_ffn",
    )(attn, dn, x, w_mix, nw_mix, nw_pre, w_in, conv_w, conv_b, w_out, nw_post)


def _rope_tables(seq):
    half = HEAD_DIM // 2
    inv = 1.0 / (ROPE_THETA ** (jnp.arange(0, HEAD_DIM, 2, dtype=F32) / HEAD_DIM))
    ang = jnp.arange(seq, dtype=F32)[:, None] * inv[None, :]
    cos, sin = jnp.cos(ang), jnp.sin(ang)
    reps = LANES // half
    sign = jnp.tile(jnp.concatenate([-jnp.ones((half,), F32), jnp.ones((half,), F32)]), reps // 2)
    return jnp.tile(cos, (1, reps)), jnp.tile(sin, (1, reps)) * sign[None, :]


def _lane_rows(vals, offset):
    depth, n = vals.shape
    return jnp.zeros((depth, 1, LANES), F32).at[:, 0, offset:offset + n].set(vals.astype(F32))


def kernel(x, w_in, dn_conv_w, dn_a_log, dn_dt_bias, dn_norm_w, w_out, ffn_w_in, ffn_conv_w,
           ffn_conv_b, ffn_w_out, norm_pre_mix, norm_post_mix, norm_pre_ffn, norm_post_ffn):
    batch, seq, d = x.shape
    depth = w_in.shape[0]
    assert w_in.shape[2] == MAIN_COLS + GATE_COLS
    cos, sin_signed = _rope_tables(seq)
    xt = x.reshape(batch * seq, d)
    row = lambda p: p[:, None, :]
    w_in_b = jnp.pad(w_in.astype(BF16), ((0, 0), (0, 0), (0, LANES - GATE_COLS)))
    w_out_b = w_out.astype(BF16)
    ffn_w_in_b = ffn_w_in.astype(BF16)
    ffn_w_out_b = ffn_w_out.astype(BF16)
    alog = _lane_rows(dn_a_log, N_DELTA_HEADS)
    dtb = _lane_rows(dn_dt_bias, N_DELTA_HEADS)
    for l in range(depth):
        proj, gates = _inproj(xt, row(norm_pre_mix), w_in_b, dn_conv_w, alog, dtb, seq, l)
        attn = _attention(proj, cos, sin_signed, batch, seq)
        dn = _delta(proj, gates, row(dn_norm_w), batch, seq, l)
        xt = _ffn(attn, dn, xt, w_out_b, row(norm_post_mix), row(norm_pre_ffn), ffn_w_in_b,
                  ffn_conv_w, row(ffn_conv_b), ffn_w_out_b, row(norm_post_ffn), seq, l)
    return xt.reshape(batch, seq, d)
```
